```python
import math
import jax, jax.numpy as jnp
from jax import lax
import numpy as np

D_MODEL = 1024
BATCH = 8
SEQ = 2048
DEPTH = 2

CHUNK = 64
D_MIX = D_MODEL
HGRN_WIDTH = D_MIX // 2
HGRN_EXPAND = 128
HGRN_HEADS = HGRN_WIDTH // HGRN_EXPAND
HGRN_DK = HGRN_EXPAND
HGRN_DV = HGRN_WIDTH // HGRN_HEADS
ATTN_WIDTH = D_MIX - HGRN_WIDTH
ATTN_HEAD_DIM = 64
ATTN_HEADS = ATTN_WIDTH // ATTN_HEAD_DIM
LEFT_CHUNKS = 8
BAND = (LEFT_CHUNKS + 1) * CHUNK
MAX_REL = 128
N_REL = 2 * MAX_REL + 1
N_GROUPS = 4
EXPERTS_PER_GROUP = 8
N_EXPERTS = N_GROUPS * EXPERTS_PER_GROUP
EXPERT_FF = D_MODEL // 4
TOP_K = 2
EPS = 1e-6
IN_SPLITS = [HGRN_WIDTH, 2 * HGRN_WIDTH, 3 * HGRN_WIDTH, 4 * HGRN_WIDTH,
             4 * HGRN_WIDTH + ATTN_WIDTH, 4 * HGRN_WIDTH + 2 * ATTN_WIDTH]
IN_COLS = 4 * HGRN_WIDTH + 3 * ATTN_WIDTH

kernel_name = "hymba_style_hgrn2_band_attn_hier_moe"


def rms_norm(x, w):
    xf = x.astype(jnp.float32)
    y = xf * lax.rsqrt(jnp.mean(xf * xf, axis=-1, keepdims=True) + EPS)
    return (y * w.astype(jnp.float32)).astype(x.dtype)


def hgrn2_mixer(q_pre, f_pre, i_in, g_pre, lb, gnorm_w):
    B, S, _ = q_pre.shape
    nC = S // CHUNK
    f32 = jnp.float32
    q = jax.nn.silu(q_pre.astype(f32))
    lbf = lb.astype(f32)
    log_f = jnp.logaddexp(jnp.log(lbf), jnp.log1p(-lbf) + jax.nn.log_sigmoid(f_pre.astype(f32)))
    k = -jnp.expm1(log_f)
    v = i_in.astype(f32)

    def to_chunks(t, d):
        return t.reshape(B, nC, CHUNK, HGRN_HEADS, d).transpose(1, 0, 3, 2, 4)

    qc, kc, vc, lfc = (to_chunks(q, HGRN_DK), to_chunks(k, HGRN_DK),
                       to_chunks(v, HGRN_DV), to_chunks(log_f, HGRN_DK))
    causal = jnp.tril(jnp.ones((CHUNK, CHUNK), dtype=bool))

    def step(state, inp):
        qb, kb, vb, lfb = inp
        bcum = jnp.cumsum(lfb, axis=2)
        diff = bcum[:, :, :, None, :] - bcum[:, :, None, :, :]
        decay = jnp.exp(jnp.where(causal[None, None, :, :, None], diff, -jnp.inf))
        scores = jnp.einsum('bhtd,bhsd,bhtsd->bhts', qb, kb, decay)
        o_intra = jnp.einsum('bhts,bhsv->bhtv', scores, vb)
        o_inter = jnp.einsum('bhtd,bhdv->bhtv', qb * jnp.exp(bcum), state)
        b_last = bcum[:, :, -1:, :]
        k_dec = kb * jnp.exp(b_last - bcum)
        new_state = jnp.exp(b_last[:, :, 0, :])[..., None] * state + \
            jnp.einsum('bhsd,bhsv->bhdv', k_dec, vb)
        return new_state, o_intra + o_inter

    s0 = jnp.zeros((B, HGRN_HEADS, HGRN_DK, HGRN_DV), f32)
    _, oc = lax.scan(step, s0, (qc, kc, vc, lfc))
    o = oc.transpose(1, 0, 3, 2, 4).reshape(B, S, HGRN_HEADS, HGRN_DV)
    o = o * lax.rsqrt(jnp.mean(o * o, axis=-1, keepdims=True) + EPS)
    o = o * gnorm_w.astype(f32).reshape(HGRN_HEADS, HGRN_DV)
    o = o.reshape(B, S, HGRN_WIDTH) * jax.nn.silu(g_pre.astype(f32))
    return o.astype(q_pre.dtype)


def chunked_band_attention(q, k, v, rel_bias, norm_w):
    B, S, _ = q.shape
    nC = S // CHUNK
    qh = q.reshape(B, nC, CHUNK, ATTN_HEADS, ATTN_HEAD_DIM)
    kh = k.reshape(B, nC, CHUNK, ATTN_HEADS, ATTN_HEAD_DIM)
    vh = v.reshape(B, nC, CHUNK, ATTN_HEADS, ATTN_HEAD_DIM)
    pad = ((0, 0), (LEFT_CHUNKS, 0), (0, 0), (0, 0), (0, 0))
    kp, vp = jnp.pad(kh, pad), jnp.pad(vh, pad)
    k_band = jnp.concatenate([kp[:, j:j + nC] for j in range(LEFT_CHUNKS + 1)], axis=2)
    v_band = jnp.concatenate([vp[:, j:j + nC] for j in range(LEFT_CHUNKS + 1)], axis=2)
    a = jnp.arange(CHUNK)[:, None]
    kk = jnp.arange(BAND)[None, :]
    rel_idx = jnp.clip(LEFT_CHUNKS * CHUNK + a - kk, -MAX_REL, MAX_REL) + MAX_REL
    bias = rel_bias.astype(jnp.float32)[:, rel_idx]
    valid = (jnp.arange(nC)[:, None] - LEFT_CHUNKS + kk // CHUNK) >= 0
    scale = 1.0 / math.sqrt(ATTN_HEAD_DIM)
    s = jnp.einsum('bnqhd,bnkhd->bhnqk', qh, k_band).astype(jnp.float32) * scale
    s = s + bias[:, None, :, :]
    s = jnp.where(valid[None, None, :, None, :], s, -jnp.inf)
    p = jax.nn.softmax(s, axis=-1).astype(v.dtype)
    o = jnp.einsum('bhnqk,bnkhd->bnqhd', p, v_band).reshape(B, S, ATTN_HEADS, ATTN_HEAD_DIM)
    of = o.astype(jnp.float32)
    of = of * lax.rsqrt(jnp.mean(of * of, axis=-1, keepdims=True) + EPS)
    of = of * norm_w.astype(jnp.float32).reshape(ATTN_HEADS, ATTN_HEAD_DIM)
    return of.reshape(B, S, ATTN_WIDTH).astype(q.dtype)


def hierarchical_moe(h, w_group, w_expert, w1, w3, w2):
    B, S, D = h.shape
    ht = h.reshape(B * S, D)
    g_logits = jnp.einsum('td,dg->tg', ht, w_group).astype(jnp.float32)
    g_prob = jax.nn.softmax(g_logits, axis=-1)
    g_gate, g_idx = lax.top_k(g_prob, 1)
    e_logits = jnp.einsum('td,de->te', ht, w_expert).astype(jnp.float32)
    e_logits = e_logits.reshape(-1, N_GROUPS, EXPERTS_PER_GROUP)
    e_in = jnp.take_along_axis(e_logits, g_idx[:, :, None], axis=1)[:, 0]
    top_v, top_i = lax.top_k(e_in, TOP_K)
    gate = jax.nn.softmax(top_v, axis=-1) * g_gate
    expert_id = g_idx * EXPERTS_PER_GROUP + top_i
    combine = jnp.sum(jax.nn.one_hot(expert_id, N_EXPERTS, dtype=jnp.float32) * gate[..., None], axis=1)
    hidden = jax.nn.silu(jnp.einsum('td,edf->tef', ht, w1)) * jnp.einsum('td,edf->tef', ht, w3)
    hidden = hidden * combine[:, :, None].astype(hidden.dtype)
    y = jnp.einsum('tef,efd->td', hidden, w2)
    return y.reshape(B, S, D).astype(h.dtype)


def setup_inputs(seed: int = 0) -> dict:
    key = jax.random.key(seed)
    ks = jax.random.split(key, 16)
    f32 = jnp.float32
    nrm = lambda k, shape, s: jax.random.normal(k, shape, f32) * s
    return {
        "x": nrm(ks[0], (BATCH, SEQ, D_MODEL), 1.0),
        "norm_mix_w": 1.0 + nrm(ks[1], (DEPTH, D_MODEL), 0.02),
        "w_in": nrm(ks[2], (DEPTH, D_MODEL, IN_COLS), D_MODEL ** -0.5),
        "hgrn_lower_bounds": nrm(ks[3], (DEPTH, HGRN_WIDTH), 0.5),
        "hgrn_gnorm_w": 1.0 + nrm(ks[4], (DEPTH, HGRN_WIDTH), 0.02),
        "attn_rel_bias": nrm(ks[5], (DEPTH, ATTN_HEADS, N_REL), 0.1),
        "attn_norm_w": 1.0 + nrm(ks[6], (DEPTH, ATTN_WIDTH), 0.02),
        "w_out": nrm(ks[7], (DEPTH, D_MIX, D_MODEL), D_MIX ** -0.5),
        "norm_ffn_w": 1.0 + nrm(ks[8], (DEPTH, D_MODEL), 0.02),
        "moe_w_group": nrm(ks[9], (DEPTH, D_MODEL, N_GROUPS), D_MODEL ** -0.5),
        "moe_w_expert": nrm(ks[10], (DEPTH, D_MODEL, N_EXPERTS), D_MODEL ** -0.5),
        "moe_w1": nrm(ks[11], (DEPTH, N_EXPERTS, D_MODEL, EXPERT_FF), D_MODEL ** -0.5),
        "moe_w3": nrm(ks[12], (DEPTH, N_EXPERTS, D_MODEL, EXPERT_FF), D_MODEL ** -0.5),
        "moe_w2": nrm(ks[13], (DEPTH, N_EXPERTS, EXPERT_FF, D_MODEL), EXPERT_FF ** -0.5),
        "final_norm_w": 1.0 + nrm(ks[14], (D_MODEL,), 0.02),
    }


def reference(x, norm_mix_w, w_in, hgrn_lower_bounds, hgrn_gnorm_w, attn_rel_bias, attn_norm_w,
              w_out, norm_ffn_w, moe_w_group, moe_w_expert, moe_w1, moe_w3, moe_w2, final_norm_w):
    lb_all = jnp.cumsum(jax.nn.softmax(hgrn_lower_bounds.astype(jnp.float32), axis=0), axis=0)
    lb_all = lb_all - lb_all[0]
    for l in range(DEPTH):
        h = rms_norm(x, norm_mix_w[l])
        proj = jnp.einsum('bsd,dc->bsc', h, w_in[l])
        hq, hf, hi, hg, aq, ak, av = jnp.split(proj, IN_SPLITS, axis=-1)
        y_rec = hgrn2_mixer(hq, hf, hi, hg, lb_all[l], hgrn_gnorm_w[l])
        y_att = chunked_band_attention(aq, ak, av, attn_rel_bias[l], attn_norm_w[l])
        mixed = jnp.concatenate([y_rec, y_att], axis=-1)
        x = x + jnp.einsum('bsc,cd->bsd', mixed, w_out[l])
        h = rms_norm(x, norm_ffn_w[l])
        x = x + hierarchical_moe(h, moe_w_group[l], moe_w_expert[l], moe_w1[l], moe_w3[l], moe_w2[l])
    return rms_norm(x, final_norm_w)
```

```python
import functools
import math

import jax
import jax.numpy as jnp
import numpy as np
from jax import lax
from jax.experimental import pallas as pl
from jax.experimental.pallas import tpu as pltpu

F32 = jnp.float32
BF16 = jnp.bfloat16

D_MODEL = 1024
DEPTH = 2
CHUNK = 64
HGRN_WIDTH = 512
HGRN_HEADS = 4
HGRN_D = 128
ATTN_WIDTH = 512
ATTN_HEADS = 8
ATTN_HEAD_DIM = 64
LEFT_CHUNKS = 8
MAX_REL = 128
N_GROUPS = 4
EXPERTS_PER_GROUP = 8
N_EXPERTS = 32
EXPERT_FF = 256
IN_COLS = 4 * HGRN_WIDTH + 3 * ATTN_WIDTH
EPS = 1e-6

LANES = 128
VMEM_LIMIT = 56 * 1024 * 1024
NEG_BIG = -1e30

TOK_TILE = 512
ATTN_QB = 256
ATTN_KB = 3 * ATTN_QB
HGRN_TB = 512
MOE_TILE = 1024
MOE_EPB = 4


def _cparams(sem):
    return pltpu.CompilerParams(dimension_semantics=sem, vmem_limit_bytes=VMEM_LIMIT)


def _silu(x):
    return x * jax.nn.sigmoid(x)


def _norm_inproj_kernel(x_ref, nw_ref, w_ref, la_ref, lc_ref,
                        q_ref, lf_ref, v_ref, g_ref, aq_ref, ak_ref, av_ref):
    x = x_ref[...]
    ms = jnp.mean(x * x, axis=-1, keepdims=True)
    h = (x * lax.rsqrt(ms + EPS) * nw_ref[...]).astype(BF16)

    def proj(j):
        return jnp.dot(h, w_ref[:, j * 512:(j + 1) * 512], preferred_element_type=F32)

    q_ref[...] = _silu(proj(0)).astype(BF16)
    z = proj(1)
    log_sig = jnp.minimum(z, 0.0) - jnp.log(1.0 + jnp.exp(-jnp.abs(z)))
    a = la_ref[...]
    b = lc_ref[...] + log_sig
    lf_ref[...] = jnp.maximum(a, b) + jnp.log(1.0 + jnp.exp(-jnp.abs(a - b)))
    v_ref[...] = proj(2).astype(BF16)
    g_ref[...] = _silu(proj(3)).astype(BF16)
    aq_ref[...] = (proj(4) * (1.0 / math.sqrt(ATTN_HEAD_DIM))).astype(BF16)
    ak_ref[...] = proj(5).astype(BF16)
    av_ref[...] = proj(6).astype(BF16)


def _norm_inproj(x2d, nw, w_in_bf16, log_lb, log_1m_lb):
    T = x2d.shape[0]
    row = lambda i: (i, 0)
    const = lambda i: (0, 0)
    out_bf = jax.ShapeDtypeStruct((T, 512), BF16)
    out_f32 = jax.ShapeDtypeStruct((T, 512), F32)
    blk = pl.BlockSpec((TOK_TILE, 512), row)
    return pl.pallas_call(
        _norm_inproj_kernel,
        grid=(T // TOK_TILE,),
        in_specs=[pl.BlockSpec((TOK_TILE, D_MODEL), row),
                  pl.BlockSpec((1, D_MODEL), const),
                  pl.BlockSpec((D_MODEL, IN_COLS), const),
                  pl.BlockSpec((1, 512), const),
                  pl.BlockSpec((1, 512), const)],
        out_specs=[blk] * 7,
        out_shape=[out_bf, out_f32, out_bf, out_bf, out_bf, out_bf, out_bf],
        compiler_params=_cparams(("parallel",)),
        name="norm_inproj",
    )(x2d, nw, w_in_bf16, log_lb, log_1m_lb)


def _chunk_cumsum(lf, row):
    acc = lf
    for s in (1, 2, 4, 8, 16, 32):
        acc = acc + jnp.where(row >= s, pltpu.roll(acc, s, 0), 0.0)
    return acc


def _level_ref(bcum, half):
    if half >= 8:
        b3 = bcum.reshape(CHUNK // (2 * half), 2 * half, HGRN_D)
        ref = jnp.broadcast_to(b3[:, half - 1:half, :], b3.shape)
        return ref.reshape(CHUNK, HGRN_D)
    b3 = bcum.reshape(CHUNK // 8, 8, HGRN_D)
    if half == 4:
        ref = jnp.broadcast_to(b3[:, 3:4, :], b3.shape)
    else:
        sub = lax.broadcasted_iota(jnp.int32, b3.shape, 1)
        ref = jnp.where(sub < 4,
                        jnp.broadcast_to(b3[:, 1:2, :], b3.shape),
                        jnp.broadcast_to(b3[:, 5:6, :], b3.shape))
    return ref.reshape(CHUNK, HGRN_D)


def _dot_nt(a, b):
    return lax.dot_general(a, b, (((1,), (1,)), ((), ())), preferred_element_type=F32)


def _dot_tn(a, b):
    return lax.dot_general(a, b, (((0,), (0,)), ((), ())), preferred_element_type=F32)


def _hgrn_kernel(q_ref, lf_ref, v_ref, g_ref, gw_ref, o_ref, st_ref):
    @pl.when(pl.program_id(1) == 0)
    def _():
        st_ref[...] = jnp.zeros_like(st_ref)

    row = lax.broadcasted_iota(jnp.int32, (CHUNK, HGRN_D), 0)
    r64 = lax.broadcasted_iota(jnp.int32, (CHUNK, CHUNK), 0)
    c64 = lax.broadcasted_iota(jnp.int32, (CHUNK, CHUNK), 1)

    def chunk_body(c, carry):
        r0 = pl.multiple_of(c * CHUNK, CHUNK)
        rows = pl.ds(r0, CHUNK)
        for h in range(HGRN_HEADS):
            cols = slice(h * HGRN_D, (h + 1) * HGRN_D)
            lf = lf_ref[rows, cols]
            q = q_ref[rows, cols].astype(F32)
            v = v_ref[rows, cols]
            f = jnp.exp(lf)
            k = 1.0 - f
            bcum = _chunk_cumsum(lf, row)

            scores = None
            for half in (32, 16, 8, 4, 2):
                upper = (row & half) != 0
                ref = _level_ref(bcum, half)
                fac = jnp.exp(jnp.where(upper, bcum - ref, ref - bcum))
                qa = jnp.where(upper, q * fac, 0.0).astype(BF16)
                ka = jnp.where(upper, 0.0, k * fac).astype(BF16)
                s_l = _dot_nt(qa, ka)
                if half != 32:
                    s_l = jnp.where((r64 & -(2 * half)) == (c64 & -(2 * half)), s_l, 0.0)
                scores = s_l if scores is None else scores + s_l
            odd = (row & 1) != 0
            qa = jnp.where(odd, q * f, 0.0).astype(BF16)
            ka = jnp.where(odd, 0.0, k).astype(BF16)
            s_l = _dot_nt(qa, ka)
            scores = scores + jnp.where((r64 & -2) == (c64 & -2), s_l, 0.0)

            st = st_ref[h]
            o = jnp.dot(scores.astype(BF16), v, preferred_element_type=F32)
            o = o + jnp.sum(q * k, axis=-1, keepdims=True) * v.astype(F32)
            o = o + _dot_nt((q * jnp.exp(bcum)).astype(BF16), st.astype(BF16))

            b_last = bcum[CHUNK - 1:CHUNK, :]
            k_dec = (k * jnp.exp(b_last - bcum)).astype(BF16)
            st_ref[h] = st * jnp.exp(b_last) + _dot_tn(v, k_dec)

            o = o * lax.rsqrt(jnp.mean(o * o, axis=-1, keepdims=True) + EPS)
            o = o * gw_ref[:, cols] * g_ref[rows, cols].astype(F32)
            o_ref[rows, cols] = o.astype(BF16)
        return carry

    lax.fori_loop(0, HGRN_TB // CHUNK, chunk_body, 0)


def _hgrn(q, lf, v, g, gnorm_w, batch, seq):
    blk = pl.BlockSpec((HGRN_TB, HGRN_WIDTH), lambda b, i: (b * (seq // HGRN_TB) + i, 0))
    return pl.pallas_call(
        _hgrn_kernel,
        grid=(batch, seq // HGRN_TB),
        in_specs=[blk, blk, blk, blk, pl.BlockSpec((1, HGRN_WIDTH), lambda b, i: (0, 0))],
        out_specs=blk,
        out_shape=jax.ShapeDtypeStruct(q.shape, BF16),
        scratch_shapes=[pltpu.VMEM((HGRN_HEADS, HGRN_D, HGRN_D), F32)],
        compiler_params=_cparams(("parallel", "arbitrary")),
        name="hgrn2",
    )(q, lf, v, g, gnorm_w)


def _attn_kernel(q_ref, k0_ref, k1_ref, k2_ref, v0_ref, v1_ref, v2_ref, bias_ref, nw_ref, o_ref):
    i = pl.program_id(1)
    col = lax.broadcasted_iota(jnp.int32, (1, ATTN_KB), 1)
    pad_mask = jnp.where(col >= (2 - i) * ATTN_QB, 0.0, NEG_BIG)
    lane = lax.broadcasted_iota(jnp.int32, (1, LANES), 1)
    lo = lane < ATTN_HEAD_DIM
    for p in range(ATTN_HEADS // 2):
        cols = slice(p * LANES, (p + 1) * LANES)
        qp = q_ref[:, cols]
        kp = jnp.concatenate([k0_ref[:, cols], k1_ref[:, cols], k2_ref[:, cols]], axis=0)
        vp = jnp.concatenate([v0_ref[:, cols], v1_ref[:, cols], v2_ref[:, cols]], axis=0)
        outs = []
        for hh in range(2):
            qh = jnp.where(lo if hh == 0 else ~lo, qp, jnp.zeros_like(qp))
            s = _dot_nt(qh, kp) + bias_ref[2 * p + hh] + pad_mask
            e = jnp.exp(s - jnp.max(s, axis=-1, keepdims=True))
            denom = jnp.sum(e, axis=-1, keepdims=True)
            outs.append(jnp.dot(e.astype(BF16), vp, preferred_element_type=F32) / denom)
        o = jnp.where(lo, outs[0], outs[1])
        o2 = o * o
        ms0 = jnp.sum(jnp.where(lo, o2, 0.0), axis=-1, keepdims=True)
        ms1 = jnp.sum(jnp.where(lo, 0.0, o2), axis=-1, keepdims=True)
        ms = jnp.where(lo, ms0, ms1) * (1.0 / ATTN_HEAD_DIM)
        o_ref[:, cols] = (o * lax.rsqrt(ms + EPS) * nw_ref[:, cols]).astype(BF16)


def _attention(aq, ak, av, bias_tab, norm_w, batch, seq):
    nb = seq // ATTN_QB
    qblk = pl.BlockSpec((ATTN_QB, ATTN_WIDTH), lambda b, i: (b * nb + i, 0))

    def kblk(back):
        return pl.BlockSpec((ATTN_QB, ATTN_WIDTH),
                            lambda b, i: (b * nb + jnp.maximum(i - back, 0), 0))

    return pl.pallas_call(
        _attn_kernel,
        grid=(batch, nb),
        in_specs=[qblk, kblk(2), kblk(1), kblk(0), kblk(2), kblk(1), kblk(0),
                  pl.BlockSpec((ATTN_HEADS, ATTN_QB, ATTN_KB), lambda b, i: (0, 0, 0)),
                  pl.BlockSpec((1, ATTN_WIDTH), lambda b, i: (0, 0))],
        out_specs=qblk,
        out_shape=jax.ShapeDtypeStruct(aq.shape, BF16),
        compiler_params=_cparams(("parallel", "arbitrary")),
        name="band_attn",
    )(aq, ak, ak, ak, av, av, av, bias_tab, norm_w)


def _bias_table(rel_bias):
    r = np.arange(ATTN_QB)[:, None]
    c = np.arange(ATTN_KB)[None, :]
    rel = 2 * ATTN_QB + r - c
    idx = np.clip(rel, -MAX_REL, MAX_REL) + MAX_REL
    dchunk = (2 * ATTN_QB) // CHUNK + r // CHUNK - c // CHUNK
    valid = (dchunk >= 0) & (dchunk <= LEFT_CHUNKS)
    tab = rel_bias.astype(F32)[:, idx]
    return jnp.where(jnp.asarray(valid)[None], tab, NEG_BIG)


def _outproj_router_kernel(yr_ref, ya_ref, x_ref, wo_ref, nw_ref, wr_ref,
                           x1_ref, h2_ref, comb_ref):
    acc = jnp.dot(yr_ref[...], wo_ref[0:HGRN_WIDTH, :], preferred_element_type=F32)
    acc = acc + jnp.dot(ya_ref[...], wo_ref[HGRN_WIDTH:, :], preferred_element_type=F32)
    x1 = x_ref[...] + acc
    x1_ref[...] = x1
    h = x1 * lax.rsqrt(jnp.mean(x1 * x1, axis=-1, keepdims=True) + EPS) * nw_ref[...]
    h2_ref[...] = h.astype(BF16)

    lg = jnp.dot(h, wr_ref[...], preferred_element_type=F32, precision=lax.Precision.HIGHEST)
    lane = lax.broadcasted_iota(jnp.int32, lg.shape, 1)
    lanef = lane.astype(F32)
    is_g = (lane >= N_EXPERTS) & (lane < N_EXPERTS + N_GROUPS)
    g_max = jnp.max(jnp.where(is_g, lg, -jnp.inf), axis=-1, keepdims=True)
    g_sum = jnp.sum(jnp.where(is_g, jnp.exp(lg - g_max), 0.0), axis=-1, keepdims=True)
    g_gate = 1.0 / g_sum
    g_idx = jnp.min(jnp.where(is_g & (lg == g_max), lanef - N_EXPERTS, 1e9),
                    axis=-1, keepdims=True)
    in_grp = (lane < N_EXPERTS) & ((lane // EXPERTS_PER_GROUP).astype(F32) == g_idx)
    v1 = jnp.max(jnp.where(in_grp, lg, -jnp.inf), axis=-1, keepdims=True)
    i1 = jnp.min(jnp.where(in_grp & (lg == v1), lanef, 1e9), axis=-1, keepdims=True)
    rest = in_grp & (lanef != i1)
    v2 = jnp.max(jnp.where(rest, lg, -jnp.inf), axis=-1, keepdims=True)
    i2 = jnp.min(jnp.where(rest & (lg == v2), lanef, 1e9), axis=-1, keepdims=True)
    e2 = jnp.exp(v2 - v1)
    gate1 = g_gate / (1.0 + e2)
    gate2 = gate1 * e2
    comb_ref[...] = jnp.where(lanef == i1, gate1, jnp.where(lanef == i2, gate2, 0.0))


def _outproj_router(y_rec, y_att, x2d, w_out_bf16, nw, w_router):
    T = x2d.shape[0]
    row = lambda i: (i, 0)
    const = lambda i: (0, 0)
    return pl.pallas_call(
        _outproj_router_kernel,
        grid=(T // TOK_TILE,),
        in_specs=[pl.BlockSpec((TOK_TILE, HGRN_WIDTH), row),
                  pl.BlockSpec((TOK_TILE, ATTN_WIDTH), row),
                  pl.BlockSpec((TOK_TILE, D_MODEL), row),
                  pl.BlockSpec((D_MODEL, D_MODEL), const),
                  pl.BlockSpec((1, D_MODEL), const),
                  pl.BlockSpec((D_MODEL, LANES), const)],
        out_specs=[pl.BlockSpec((TOK_TILE, D_MODEL), row),
                   pl.BlockSpec((TOK_TILE, D_MODEL), row),
                   pl.BlockSpec((TOK_TILE, LANES), row)],
        out_shape=[jax.ShapeDtypeStruct((T, D_MODEL), F32),
                   jax.ShapeDtypeStruct((T, D_MODEL), BF16),
                   jax.ShapeDtypeStruct((T, LANES), F32)],
        compiler_params=_cparams(("parallel",)),
        name="outproj_router",
    )(y_rec, y_att, x2d, w_out_bf16, nw, w_router)


def _moe_kernel(h_ref, comb_ref, x1_ref, w1_ref, w3_ref, w2_ref, fw_ref, o_ref, acc_ref,
                *, final_norm):
    c = pl.program_id(1)

    @pl.when(c == 0)
    def _():
        acc_ref[...] = jnp.zeros_like(acc_ref)

    h = h_ref[...]
    comb = comb_ref[...]
    lane = lax.broadcasted_iota(jnp.int32, comb.shape, 1)
    for j in range(MOE_EPB):
        a = jnp.dot(h, w1_ref[j], preferred_element_type=F32)
        b = jnp.dot(h, w3_ref[j], preferred_element_type=F32)
        cw = jnp.sum(jnp.where(lane == c * MOE_EPB + j, comb, 0.0), axis=-1, keepdims=True)
        hid = (_silu(a) * b * cw).astype(BF16)
        acc_ref[...] += jnp.dot(hid, w2_ref[j], preferred_element_type=F32)

    @pl.when(c == pl.num_programs(1) - 1)
    def _():
        x2 = x1_ref[...] + acc_ref[...]
        if final_norm:
            x2 = x2 * lax.rsqrt(jnp.mean(x2 * x2, axis=-1, keepdims=True) + EPS) * fw_ref[...]
        o_ref[...] = x2


def _moe(h2, comb, x1, w1, w3, w2, final_w, final_norm):
    T = h2.shape[0]
    row = lambda i, c: (i, 0)
    return pl.pallas_call(
        functools.partial(_moe_kernel, final_norm=final_norm),
        grid=(T // MOE_TILE, N_EXPERTS // MOE_EPB),
        in_specs=[pl.BlockSpec((MOE_TILE, D_MODEL), row),
                  pl.BlockSpec((MOE_TILE, LANES), row),
                  pl.BlockSpec((MOE_TILE, D_MODEL), row),
                  pl.BlockSpec((MOE_EPB, D_MODEL, EXPERT_FF), lambda i, c: (c, 0, 0)),
                  pl.BlockSpec((MOE_EPB, D_MODEL, EXPERT_FF), lambda i, c: (c, 0, 0)),
                  pl.BlockSpec((MOE_EPB, EXPERT_FF, D_MODEL), lambda i, c: (c, 0, 0)),
                  pl.BlockSpec((1, D_MODEL), lambda i, c: (0, 0))],
        out_specs=pl.BlockSpec((MOE_TILE, D_MODEL), row),
        out_shape=jax.ShapeDtypeStruct((T, D_MODEL), F32),
        scratch_shapes=[pltpu.VMEM((MOE_TILE, D_MODEL), F32)],
        compiler_params=_cparams(("parallel", "arbitrary")),
        name="moe_experts",
    )(h2, comb, x1, w1, w3, w2, final_w)


def kernel(x, norm_mix_w, w_in, hgrn_lower_bounds, hgrn_gnorm_w, attn_rel_bias, attn_norm_w,
           w_out, norm_ffn_w, moe_w_group, moe_w_expert, moe_w1, moe_w3, moe_w2, final_norm_w):
    B, S, D = x.shape
    T = B * S
    lb_all = jnp.cumsum(jax.nn.softmax(hgrn_lower_bounds.astype(F32), axis=0), axis=0)
    lb_all = lb_all - lb_all[0]
    log_lb = jnp.log(lb_all)
    log_1m_lb = jnp.log1p(-lb_all)

    xt = x.reshape(T, D)
    for l in range(DEPTH):
        q, lf, v, g, aq, ak, av = _norm_inproj(
            xt, norm_mix_w[l][None], w_in[l].astype(BF16), log_lb[l][None], log_1m_lb[l][None])
        y_rec = _hgrn(q, lf, v, g, hgrn_gnorm_w[l][None], B, S)
        y_att = _attention(aq, ak, av, _bias_table(attn_rel_bias[l]), attn_norm_w[l][None], B, S)
        w_router = jnp.concatenate(
            [moe_w_expert[l], moe_w_group[l],
             jnp.zeros((D, LANES - N_EXPERTS - N_GROUPS), F32)], axis=1)
        x1, h2, comb = _outproj_router(y_rec, y_att, xt, w_out[l].astype(BF16),
                                       norm_ffn_w[l][None], w_router)
        xt = _moe(h2, comb, x1, moe_w1[l].astype(BF16), moe_w3[l].astype(BF16),
                  moe_w2[l].astype(BF16), final_norm_w[None], final_norm=(l == DEPTH - 1))
    return xt.reshape(B, S, D)
```

```python
import functools
import math

import jax
import jax.numpy as jnp
import numpy as np
from jax import lax
from jax.experimental import pallas as pl
from jax.experimental.pallas import tpu as pltpu

F32 = jnp.float32
BF16 = jnp.bfloat16

D_MODEL = 1024
DEPTH = 2
CHUNK = 64
HGRN_WIDTH = 512
HGRN_HEADS = 4
HGRN_D = 128
ATTN_WIDTH = 512
ATTN_HEADS = 8
ATTN_HEAD_DIM = 64
LEFT_CHUNKS = 8
MAX_REL = 128
N_GROUPS = 4
EXPERTS_PER_GROUP = 8
N_EXPERTS = 32
EXPERT_FF = 256
IN_COLS = 4 * HGRN_WIDTH + 3 * ATTN_WIDTH
EPS = 1e-6

LANES = 128
VMEM_LIMIT = 56 * 1024 * 1024
NEG_BIG = -1e30

TOK_TILE = 512
ATTN_QB = 256
ATTN_KB = 3 * ATTN_QB
HGRN_TB = 512
MOE_TILE = 1024
MOE_EPB = 4


def _cparams(sem):
    return pltpu.CompilerParams(dimension_semantics=sem, vmem_limit_bytes=VMEM_LIMIT)


def _silu(x):
    return x * jax.nn.sigmoid(x)


def _norm_inproj_kernel(x_ref, nw_ref, w_ref, la_ref, lc_ref,
                        q_ref, lf_ref, v_ref, g_ref, aq_ref, ak_ref, av_ref):
    x = x_ref[...]
    ms = jnp.mean(x * x, axis=-1, keepdims=True)
    h = (x * lax.rsqrt(ms + EPS) * nw_ref[...]).astype(BF16)

    def proj(j):
        return jnp.dot(h, w_ref[:, j * 512:(j + 1) * 512], preferred_element_type=F32)

    q_ref[...] = _silu(proj(0)).astype(BF16)
    z = proj(1)
    log_sig = jnp.minimum(z, 0.0) - jnp.log(1.0 + jnp.exp(-jnp.abs(z)))
    a = la_ref[...]
    b = lc_ref[...] + log_sig
    lf_ref[...] = jnp.maximum(a, b) + jnp.log(1.0 + jnp.exp(-jnp.abs(a - b)))
    v_ref[...] = proj(2).astype(BF16)
    g_ref[...] = _silu(proj(3)).astype(BF16)
    aq_ref[...] = (proj(4) * (1.0 / math.sqrt(ATTN_HEAD_DIM))).astype(BF16)
    ak_ref[...] = proj(5).astype(BF16)
    av_ref[...] = proj(6).astype(BF16)


def _norm_inproj(x2d, nw, w_in_bf16, log_lb, log_1m_lb):
    T = x2d.shape[0]
    row = lambda i: (i, 0)
    const = lambda i: (0, 0)
    out_bf = jax.ShapeDtypeStruct((T, 512), BF16)
    out_f32 = jax.ShapeDtypeStruct((T, 512), F32)
    blk = pl.BlockSpec((TOK_TILE, 512), row)
    return pl.pallas_call(
        _norm_inproj_kernel,
        grid=(T // TOK_TILE,),
        in_specs=[pl.BlockSpec((TOK_TILE, D_MODEL), row),
                  pl.BlockSpec((1, D_MODEL), const),
                  pl.BlockSpec((D_MODEL, IN_COLS), const),
                  pl.BlockSpec((1, 512), const),
                  pl.BlockSpec((1, 512), const)],
        out_specs=[blk] * 7,
        out_shape=[out_bf, out_f32, out_bf, out_bf, out_bf, out_bf, out_bf],
        compiler_params=_cparams(("parallel",)),
        name="norm_inproj",
    )(x2d, nw, w_in_bf16, log_lb, log_1m_lb)


def _chunk_cumsum(lf, row):
    acc = lf
    for s in (1, 2, 4, 8, 16, 32):
        acc = acc + jnp.where(row >= s, pltpu.roll(acc, s, 0), 0.0)
    return acc


def _level_ref(bcum, half):
    if half >= 8:
        b3 = bcum.reshape(CHUNK // (2 * half), 2 * half, HGRN_D)
        ref = jnp.broadcast_to(b3[:, half - 1:half, :], b3.shape)
        return ref.reshape(CHUNK, HGRN_D)
    b3 = bcum.reshape(CHUNK // 8, 8, HGRN_D)
    if half == 4:
        ref = jnp.broadcast_to(b3[:, 3:4, :], b3.shape)
    else:
        sub = lax.broadcasted_iota(jnp.int32, b3.shape, 1)
        ref = jnp.where(sub < 4,
                        jnp.broadcast_to(b3[:, 1:2, :], b3.shape),
                        jnp.broadcast_to(b3[:, 5:6, :], b3.shape))
    return ref.reshape(CHUNK, HGRN_D)


def _dot_nt(a, b):
    return lax.dot_general(a, b, (((1,), (1,)), ((), ())), preferred_element_type=F32)


def _dot_tn(a, b):
    return lax.dot_general(a, b, (((0,), (0,)), ((), ())), preferred_element_type=F32)


def _hgrn_kernel(q_ref, lf_ref, v_ref, g_ref, gw_ref, o_ref, st_ref):
    @pl.when(pl.program_id(1) == 0)
    def _():
        st_ref[...] = jnp.zeros_like(st_ref)

    row = lax.broadcasted_iota(jnp.int32, (CHUNK, HGRN_D), 0)
    r64 = lax.broadcasted_iota(jnp.int32, (CHUNK, CHUNK), 0)
    c64 = lax.broadcasted_iota(jnp.int32, (CHUNK, CHUNK), 1)

    def chunk_body(c, carry):
        r0 = pl.multiple_of(c * CHUNK, CHUNK)
        rows = pl.ds(r0, CHUNK)
        for h in range(HGRN_HEADS):
            cols = slice(h * HGRN_D, (h + 1) * HGRN_D)
            lf = lf_ref[rows, cols]
            q = q_ref[rows, cols].astype(F32)
            v = v_ref[rows, cols]
            f = jnp.exp(lf)
            k = 1.0 - f
            bcum = _chunk_cumsum(lf, row)

            scores = None
            for half in (32, 16, 8, 4, 2):
                upper = (row & half) != 0
                ref = _level_ref(bcum, half)
                fac = jnp.exp(jnp.where(upper, bcum - ref, ref - bcum))
                qa = jnp.where(upper, q * fac, 0.0).astype(BF16)
                ka = jnp.where(upper, 0.0, k * fac).astype(BF16)
                s_l = _dot_nt(qa, ka)
                if half != 32:
                    s_l = jnp.where((r64 & -(2 * half)) == (c64 & -(2 * half)), s_l, 0.0)
                scores = s_l if scores is None else scores + s_l
            odd = (row & 1) != 0
            qa = jnp.where(odd, q * f, 0.0).astype(BF16)
            ka = jnp.where(odd, 0.0, k).astype(BF16)
            s_l = _dot_nt(qa, ka)
            scores = scores + jnp.where((r64 & -2) == (c64 & -2), s_l, 0.0)

            st = st_ref[h]
            o = jnp.dot(scores.astype(BF16), v, preferred_element_type=F32)
            o = o + jnp.sum(q * k, axis=-1, keepdims=True) * v.astype(F32)
            o = o + _dot_nt((q * jnp.exp(bcum)).astype(BF16), st.astype(BF16))

            b_last = bcum[CHUNK - 1:CHUNK, :]
            k_dec = (k * jnp.exp(b_last - bcum)).astype(BF16)
            st_ref[h] = st * jnp.exp(b_last) + _dot_tn(v, k_dec)

            o = o * lax.rsqrt(jnp.mean(o * o, axis=-1, keepdims=True) + EPS)
            o = o * gw_ref[:, cols] * g_ref[rows, cols].astype(F32)
            o_ref[rows, cols] = o.astype(BF16)
        return carry

    lax.fori_loop(0, HGRN_TB // CHUNK, chunk_body, 0)


def _hgrn(q, lf, v, g, gnorm_w, batch, seq):
    blk = pl.BlockSpec((HGRN_TB, HGRN_WIDTH), lambda b, i: (b * (seq // HGRN_TB) + i, 0))
    return pl.pallas_call(
        _hgrn_kernel,
        grid=(batch, seq // HGRN_TB),
        in_specs=[blk, blk, blk, blk, pl.BlockSpec((1, HGRN_WIDTH), lambda b, i: (0, 0))],
        out_specs=blk,
        out_shape=jax.ShapeDtypeStruct(q.shape, BF16),
        scratch_shapes=[pltpu.VMEM((HGRN_HEADS, HGRN_D, HGRN_D), F32)],
        compiler_params=_cparams(("parallel", "arbitrary")),
        name="hgrn2",
    )(q, lf, v, g, gnorm_w)


def _attn_kernel(q_ref, k0_ref, k1_ref, k2_ref, v0_ref, v1_ref, v2_ref, bias_ref, nw_ref, o_ref):
    i = pl.program_id(1)
    col = lax.broadcasted_iota(jnp.int32, (1, ATTN_KB), 1)
    pad_mask = jnp.where(col >= (2 - i) * ATTN_QB, 0.0, NEG_BIG)
    lane = lax.broadcasted_iota(jnp.int32, (1, LANES), 1)
    lo = lane < ATTN_HEAD_DIM
    for p in range(ATTN_HEADS // 2):
        cols = slice(p * LANES, (p + 1) * LANES)
        qp = q_ref[:, cols]
        kp = jnp.concatenate([k0_ref[:, cols], k1_ref[:, cols], k2_ref[:, cols]], axis=0)
        vp = jnp.concatenate([v0_ref[:, cols], v1_ref[:, cols], v2_ref[:, cols]], axis=0)
        outs = []
        for hh in range(2):
            qh = jnp.where(lo if hh == 0 else ~lo, qp, jnp.zeros_like(qp))
            s = _dot_nt(qh, kp) + bias_ref[2 * p + hh] + pad_mask
            e = jnp.exp(s - jnp.max(s, axis=-1, keepdims=True))
            denom = jnp.sum(e, axis=-1, keepdims=True)
            outs.append(jnp.dot(e.astype(BF16), vp, preferred_element_type=F32) / denom)
        o = jnp.where(lo, outs[0], outs[1])
        o2 = o * o
        ms0 = jnp.sum(jnp.where(lo, o2, 0.0), axis=-1, keepdims=True)
        ms1 = jnp.sum(jnp.where(lo, 0.0, o2), axis=-1, keepdims=True)
        ms = jnp.where(lo, ms0, ms1) * (1.0 / ATTN_HEAD_DIM)
        o_ref[:, cols] = (o * lax.rsqrt(ms + EPS) * nw_ref[:, cols]).astype(BF16)


def _attention(aq, ak, av, bias_tab, norm_w, batch, seq):
    nb = seq // ATTN_QB
    qblk = pl.BlockSpec((ATTN_QB, ATTN_WIDTH), lambda b, i: (b * nb + i, 0))

    def kblk(back):
        return pl.BlockSpec((ATTN_QB, ATTN_WIDTH),
                            lambda b, i: (b * nb + jnp.maximum(i - back, 0), 0))

    return pl.pallas_call(
        _attn_kernel,
        grid=(batch, nb),
        in_specs=[qblk, kblk(2), kblk(1), kblk(0), kblk(2), kblk(1), kblk(0),
                  pl.BlockSpec((ATTN_HEADS, ATTN_QB, ATTN_KB), lambda b, i: (0, 0, 0)),
                  pl.BlockSpec((1, ATTN_WIDTH), lambda b, i: (0, 0))],
        out_specs=qblk,
        out_shape=jax.ShapeDtypeStruct(aq.shape, BF16),
        compiler_params=_cparams(("parallel", "arbitrary")),
        name="band_attn",
    )(aq, ak, ak, ak, av, av, av, bias_tab, norm_w)


BIAS_ROW = ATTN_QB + ATTN_KB


def _bias_kernel(base_ref, o_ref):
    row = lax.broadcasted_iota(jnp.int32, (ATTN_QB, BIAS_ROW), 0)
    x = jnp.broadcast_to(base_ref[0], (ATTN_QB, BIAS_ROW))
    x = pltpu.roll(x, BIAS_ROW - ATTN_QB, 1)
    for bit in range(int(math.log2(ATTN_QB))):
        x = jnp.where(((row >> bit) & 1) == 1, pltpu.roll(x, 1 << bit, 1), x)
    r = lax.broadcasted_iota(jnp.int32, (ATTN_QB, ATTN_KB), 0)
    c = lax.broadcasted_iota(jnp.int32, (ATTN_QB, ATTN_KB), 1)
    dchunk = (2 * ATTN_QB) // CHUNK + r // CHUNK - c // CHUNK
    valid = (dchunk >= 0) & (dchunk <= LEFT_CHUNKS)
    o_ref[0] = jnp.where(valid, x[:, :ATTN_KB], NEG_BIG)


def _bias_table(rel_bias):
    H = rel_bias.shape[0]
    n_hi = 2 * ATTN_QB - MAX_REL + ATTN_QB
    n_lo = BIAS_ROW - n_hi - (2 * MAX_REL + 1)
    rb = rel_bias.astype(F32)
    base = jnp.concatenate([jnp.broadcast_to(rb[:, -1:], (H, n_hi)), rb[:, ::-1],
                            jnp.broadcast_to(rb[:, :1], (H, n_lo))], axis=1)
    return pl.pallas_call(
        _bias_kernel,
        grid=(H,),
        in_specs=[pl.BlockSpec((1, 1, BIAS_ROW), lambda h: (h, 0, 0))],
        out_specs=pl.BlockSpec((1, ATTN_QB, ATTN_KB), lambda h: (h, 0, 0)),
        out_shape=jax.ShapeDtypeStruct((H, ATTN_QB, ATTN_KB), F32),
        compiler_params=_cparams(("parallel",)),
        name="rel_bias_table",
    )(base.reshape(H, 1, BIAS_ROW))


def _outproj_router_kernel(yr_ref, ya_ref, x_ref, wo_ref, nw_ref, wr_ref,
                           x1_ref, h2_ref, comb_ref):
    acc = jnp.dot(yr_ref[...], wo_ref[0:HGRN_WIDTH, :], preferred_element_type=F32)
    acc = acc + jnp.dot(ya_ref[...], wo_ref[HGRN_WIDTH:, :], preferred_element_type=F32)
    x1 = x_ref[...] + acc
    x1_ref[...] = x1
    h = x1 * lax.rsqrt(jnp.mean(x1 * x1, axis=-1, keepdims=True) + EPS) * nw_ref[...]
    h_hi = h.astype(BF16)
    h2_ref[...] = h_hi

    h_lo = (h - h_hi.astype(F32)).astype(BF16)
    hi_prod = jnp.dot(h_hi, wr_ref[...], preferred_element_type=F32)
    lg = (hi_prod[:, :LANES] + hi_prod[:, LANES:]
          + jnp.dot(h_lo, wr_ref[:, :LANES], preferred_element_type=F32))
    lane = lax.broadcasted_iota(jnp.int32, lg.shape, 1)
    lanef = lane.astype(F32)
    is_g = (lane >= N_EXPERTS) & (lane < N_EXPERTS + N_GROUPS)
    g_max = jnp.max(jnp.where(is_g, lg, -jnp.inf), axis=-1, keepdims=True)
    g_sum = jnp.sum(jnp.where(is_g, jnp.exp(lg - g_max), 0.0), axis=-1, keepdims=True)
    g_gate = 1.0 / g_sum
    g_idx = jnp.min(jnp.where(is_g & (lg == g_max), lanef - N_EXPERTS, 1e9),
                    axis=-1, keepdims=True)
    in_grp = (lane < N_EXPERTS) & ((lane // EXPERTS_PER_GROUP).astype(F32) == g_idx)
    v1 = jnp.max(jnp.where(in_grp, lg, -jnp.inf), axis=-1, keepdims=True)
    i1 = jnp.min(jnp.where(in_grp & (lg == v1), lanef, 1e9), axis=-1, keepdims=True)
    rest = in_grp & (lanef != i1)
    v2 = jnp.max(jnp.where(rest, lg, -jnp.inf), axis=-1, keepdims=True)
    i2 = jnp.min(jnp.where(rest & (lg == v2), lanef, 1e9), axis=-1, keepdims=True)
    e2 = jnp.exp(v2 - v1)
    gate1 = g_gate / (1.0 + e2)
    gate2 = gate1 * e2
    comb_ref[...] = jnp.where(lanef == i1, gate1, jnp.where(lanef == i2, gate2, 0.0))


def _outproj_router(y_rec, y_att, x2d, w_out_bf16, nw, w_router):
    T = x2d.shape[0]
    row = lambda i: (i, 0)
    const = lambda i: (0, 0)
    return pl.pallas_call(
        _outproj_router_kernel,
        grid=(T // TOK_TILE,),
        in_specs=[pl.BlockSpec((TOK_TILE, HGRN_WIDTH), row),
                  pl.BlockSpec((TOK_TILE, ATTN_WIDTH), row),
                  pl.BlockSpec((TOK_TILE, D_MODEL), row),
                  pl.BlockSpec((D_MODEL, D_MODEL), const),
                  pl.BlockSpec((1, D_MODEL), const),
                  pl.BlockSpec((D_MODEL, 2 * LANES), const)],
        out_specs=[pl.BlockSpec((TOK_TILE, D_MODEL), row),
                   pl.BlockSpec((TOK_TILE, D_MODEL), row),
                   pl.BlockSpec((TOK_TILE, LANES), row)],
        out_shape=[jax.ShapeDtypeStruct((T, D_MODEL), F32),
                   jax.ShapeDtypeStruct((T, D_MODEL), BF16),
                   jax.ShapeDtypeStruct((T, LANES), F32)],
        compiler_params=_cparams(("parallel",)),
        name="outproj_router",
    )(y_rec, y_att, x2d, w_out_bf16, nw, w_router)


def _moe_kernel(h_ref, comb_ref, x1_ref, w1_ref, w3_ref, w2_ref, fw_ref, o_ref, acc_ref,
                *, final_norm):
    c = pl.program_id(1)

    @pl.when(c == 0)
    def _():
        acc_ref[...] = jnp.zeros_like(acc_ref)

    h = h_ref[...]
    comb = comb_ref[...]
    lane = lax.broadcasted_iota(jnp.int32, comb.shape, 1)
    for j in range(MOE_EPB):
        a = jnp.dot(h, w1_ref[j], preferred_element_type=F32)
        b = jnp.dot(h, w3_ref[j], preferred_element_type=F32)
        cw = jnp.sum(jnp.where(lane == c * MOE_EPB + j, comb, 0.0), axis=-1, keepdims=True)
        hid = (_silu(a) * b * cw).astype(BF16)
        acc_ref[...] += jnp.dot(hid, w2_ref[j], preferred_element_type=F32)

    @pl.when(c == pl.num_programs(1) - 1)
    def _():
        x2 = x1_ref[...] + acc_ref[...]
        if final_norm:
            x2 = x2 * lax.rsqrt(jnp.mean(x2 * x2, axis=-1, keepdims=True) + EPS) * fw_ref[...]
        o_ref[...] = x2


def _moe(h2, comb, x1, w1, w3, w2, final_w, final_norm):
    T = h2.shape[0]
    row = lambda i, c: (i, 0)
    return pl.pallas_call(
        functools.partial(_moe_kernel, final_norm=final_norm),
        grid=(T // MOE_TILE, N_EXPERTS // MOE_EPB),
        in_specs=[pl.BlockSpec((MOE_TILE, D_MODEL), row),
                  pl.BlockSpec((MOE_TILE, LANES), row),
                  pl.BlockSpec((MOE_TILE, D_MODEL), row),
                  pl.BlockSpec((MOE_EPB, D_MODEL, EXPERT_FF), lambda i, c: (c, 0, 0)),
                  pl.BlockSpec((MOE_EPB, D_MODEL, EXPERT_FF), lambda i, c: (c, 0, 0)),
                  pl.BlockSpec((MOE_EPB, EXPERT_FF, D_MODEL), lambda i, c: (c, 0, 0)),
                  pl.BlockSpec((1, D_MODEL), lambda i, c: (0, 0))],
        out_specs=pl.BlockSpec((MOE_TILE, D_MODEL), row),
        out_shape=jax.ShapeDtypeStruct((T, D_MODEL), F32),
        scratch_shapes=[pltpu.VMEM((MOE_TILE, D_MODEL), F32)],
        compiler_params=_cparams(("parallel", "arbitrary")),
        name="moe_experts",
    )(h2, comb, x1, w1, w3, w2, final_w)


def kernel(x, norm_mix_w, w_in, hgrn_lower_bounds, hgrn_gnorm_w, attn_rel_bias, attn_norm_w,
           w_out, norm_ffn_w, moe_w_group, moe_w_expert, moe_w1, moe_w3, moe_w2, final_norm_w):
    B, S, D = x.shape
    T = B * S
    lb_all = jnp.cumsum(jax.nn.softmax(hgrn_lower_bounds.astype(F32), axis=0), axis=0)
    lb_all = lb_all - lb_all[0]
    log_lb = jnp.log(lb_all)
    log_1m_lb = jnp.log1p(-lb_all)

    xt = x.reshape(T, D)
    for l in range(DEPTH):
        q, lf, v, g, aq, ak, av = _norm_inproj(
            xt, norm_mix_w[l][None], w_in[l].astype(BF16), log_lb[l][None], log_1m_lb[l][None])
        y_rec = _hgrn(q, lf, v, g, hgrn_gnorm_w[l][None], B, S)
        y_att = _attention(aq, ak, av, _bias_table(attn_rel_bias[l]), attn_norm_w[l][None], B, S)
        w_r = jnp.concatenate(
            [moe_w_expert[l], moe_w_group[l],
             jnp.zeros((D, LANES - N_EXPERTS - N_GROUPS), F32)], axis=1)
        w_r_hi = w_r.astype(BF16)
        w_router = jnp.concatenate([w_r_hi, (w_r - w_r_hi.astype(F32)).astype(BF16)], axis=1)
        x1, h2, comb = _outproj_router(y_rec, y_att, xt, w_out[l].astype(BF16),
                                       norm_ffn_w[l][None], w_router)
        xt = _moe(h2, comb, x1, moe_w1[l].astype(BF16), moe_w3[l].astype(BF16),
                  moe_w2[l].astype(BF16), final_norm_w[None], final_norm=(l == DEPTH - 1))
    return xt.reshape(B, S, D)
```

```python
import functools
import math

import jax
import jax.numpy as jnp
import numpy as np
from jax import lax
from jax.experimental import pallas as pl
from jax.experimental.pallas import tpu as pltpu

F32 = jnp.float32
BF16 = jnp.bfloat16

D_MODEL = 1024
DEPTH = 2
CHUNK = 64
HGRN_WIDTH = 512
HGRN_HEADS = 4
HGRN_D = 128
ATTN_WIDTH = 512
ATTN_HEADS = 8
ATTN_HEAD_DIM = 64
LEFT_CHUNKS = 8
MAX_REL = 128
N_GROUPS = 4
EXPERTS_PER_GROUP = 8
N_EXPERTS = 32
EXPERT_FF = 256
IN_COLS = 4 * HGRN_WIDTH + 3 * ATTN_WIDTH
EPS = 1e-6

LANES = 128
VMEM_LIMIT = 56 * 1024 * 1024
NEG_BIG = -1e30

TOK_TILE = 512
ATTN_QB = 256
ATTN_KB = 3 * ATTN_QB
HGRN_TB = 512
TOP_K = 2
MOE_ROWS = 256
DISP_TILE = 1024
COMB_TILE = 512
R_E1, R_E2, R_RANK1, R_RANK2, R_G1, R_G2 = range(6)


def _cparams(sem):
    return pltpu.CompilerParams(dimension_semantics=sem, vmem_limit_bytes=VMEM_LIMIT)


def _silu(x):
    return x * jax.nn.sigmoid(x)


def _norm_inproj_kernel(x_ref, nw_ref, w_ref, la_ref, lc_ref,
                        q_ref, lf_ref, v_ref, g_ref, aq_ref, ak_ref, av_ref):
    x = x_ref[...]
    ms = jnp.mean(x * x, axis=-1, keepdims=True)
    h = (x * lax.rsqrt(ms + EPS) * nw_ref[...]).astype(BF16)

    def proj(j):
        return jnp.dot(h, w_ref[:, j * 512:(j + 1) * 512], preferred_element_type=F32)

    q_ref[...] = _silu(proj(0)).astype(BF16)
    z = proj(1)
    log_sig = jnp.minimum(z, 0.0) - jnp.log(1.0 + jnp.exp(-jnp.abs(z)))
    a = la_ref[...]
    b = lc_ref[...] + log_sig
    lf_ref[...] = jnp.maximum(a, b) + jnp.log(1.0 + jnp.exp(-jnp.abs(a - b)))
    v_ref[...] = proj(2).astype(BF16)
    g_ref[...] = _silu(proj(3)).astype(BF16)
    aq_ref[...] = (proj(4) * (1.0 / math.sqrt(ATTN_HEAD_DIM))).astype(BF16)
    ak_ref[...] = proj(5).astype(BF16)
    av_ref[...] = proj(6).astype(BF16)


def _norm_inproj(x2d, nw, w_in_bf16, log_lb, log_1m_lb):
    T = x2d.shape[0]
    row = lambda i: (i, 0)
    const = lambda i: (0, 0)
    out_bf = jax.ShapeDtypeStruct((T, 512), BF16)
    out_f32 = jax.ShapeDtypeStruct((T, 512), F32)
    blk = pl.BlockSpec((TOK_TILE, 512), row)
    return pl.pallas_call(
        _norm_inproj_kernel,
        grid=(T // TOK_TILE,),
        in_specs=[pl.BlockSpec((TOK_TILE, D_MODEL), row),
                  pl.BlockSpec((1, D_MODEL), const),
                  pl.BlockSpec((D_MODEL, IN_COLS), const),
                  pl.BlockSpec((1, 512), const),
                  pl.BlockSpec((1, 512), const)],
        out_specs=[blk] * 7,
        out_shape=[out_bf, out_f32, out_bf, out_bf, out_bf, out_bf, out_bf],
        compiler_params=_cparams(("parallel",)),
        name="norm_inproj",
    )(x2d, nw, w_in_bf16, log_lb, log_1m_lb)


def _chunk_cumsum(lf, row):
    acc = lf
    for s in (1, 2, 4, 8, 16, 32):
        acc = acc + jnp.where(row >= s, pltpu.roll(acc, s, 0), 0.0)
    return acc


def _level_ref(bcum, half):
    if half >= 8:
        b3 = bcum.reshape(CHUNK // (2 * half), 2 * half, HGRN_D)
        ref = jnp.broadcast_to(b3[:, half - 1:half, :], b3.shape)
        return ref.reshape(CHUNK, HGRN_D)
    b3 = bcum.reshape(CHUNK // 8, 8, HGRN_D)
    if half == 4:
        ref = jnp.broadcast_to(b3[:, 3:4, :], b3.shape)
    else:
        sub = lax.broadcasted_iota(jnp.int32, b3.shape, 1)
        ref = jnp.where(sub < 4,
                        jnp.broadcast_to(b3[:, 1:2, :], b3.shape),
                        jnp.broadcast_to(b3[:, 5:6, :], b3.shape))
    return ref.reshape(CHUNK, HGRN_D)


def _dot_nt(a, b):
    return lax.dot_general(a, b, (((1,), (1,)), ((), ())), preferred_element_type=F32)


def _dot_tn(a, b):
    return lax.dot_general(a, b, (((0,), (0,)), ((), ())), preferred_element_type=F32)


def _hgrn_kernel(q_ref, lf_ref, v_ref, g_ref, gw_ref, o_ref, st_ref):
    @pl.when(pl.program_id(1) == 0)
    def _():
        st_ref[...] = jnp.zeros_like(st_ref)

    row = lax.broadcasted_iota(jnp.int32, (CHUNK, HGRN_D), 0)
    r64 = lax.broadcasted_iota(jnp.int32, (CHUNK, CHUNK), 0)
    c64 = lax.broadcasted_iota(jnp.int32, (CHUNK, CHUNK), 1)

    def chunk_body(c, carry):
        r0 = pl.multiple_of(c * CHUNK, CHUNK)
        rows = pl.ds(r0, CHUNK)
        for h in range(HGRN_HEADS):
            cols = slice(h * HGRN_D, (h + 1) * HGRN_D)
            lf = lf_ref[rows, cols]
            q = q_ref[rows, cols].astype(F32)
            v = v_ref[rows, cols]
            f = jnp.exp(lf)
            k = 1.0 - f
            bcum = _chunk_cumsum(lf, row)

            scores = None
            for half in (32, 16, 8, 4, 2):
                upper = (row & half) != 0
                ref = _level_ref(bcum, half)
                fac = jnp.exp(jnp.where(upper, bcum - ref, ref - bcum))
                qa = jnp.where(upper, q * fac, 0.0).astype(BF16)
                ka = jnp.where(upper, 0.0, k * fac).astype(BF16)
                s_l = _dot_nt(qa, ka)
                if half != 32:
                    s_l = jnp.where((r64 & -(2 * half)) == (c64 & -(2 * half)), s_l, 0.0)
                scores = s_l if scores is None else scores + s_l
            odd = (row & 1) != 0
            qa = jnp.where(odd, q * f, 0.0).astype(BF16)
            ka = jnp.where(odd, 0.0, k).astype(BF16)
            s_l = _dot_nt(qa, ka)
            scores = scores + jnp.where((r64 & -2) == (c64 & -2), s_l, 0.0)

            st = st_ref[h]
            o = jnp.dot(scores.astype(BF16), v, preferred_element_type=F32)
            o = o + jnp.sum(q * k, axis=-1, keepdims=True) * v.astype(F32)
            o = o + _dot_nt((q * jnp.exp(bcum)).astype(BF16), st.astype(BF16))

            b_last = bcum[CHUNK - 1:CHUNK, :]
            k_dec = (k * jnp.exp(b_last - bcum)).astype(BF16)
            st_ref[h] = st * jnp.exp(b_last) + _dot_tn(v, k_dec)

            o = o * lax.rsqrt(jnp.mean(o * o, axis=-1, keepdims=True) + EPS)
            o = o * gw_ref[:, cols] * g_ref[rows, cols].astype(F32)
            o_ref[rows, cols] = o.astype(BF16)
        return carry

    lax.fori_loop(0, HGRN_TB // CHUNK, chunk_body, 0)


def _hgrn(q, lf, v, g, gnorm_w, batch, seq):
    blk = pl.BlockSpec((HGRN_TB, HGRN_WIDTH), lambda b, i: (b * (seq // HGRN_TB) + i, 0))
    return pl.pallas_call(
        _hgrn_kernel,
        grid=(batch, seq // HGRN_TB),
        in_specs=[blk, blk, blk, blk, pl.BlockSpec((1, HGRN_WIDTH), lambda b, i: (0, 0))],
        out_specs=blk,
        out_shape=jax.ShapeDtypeStruct(q.shape, BF16),
        scratch_shapes=[pltpu.VMEM((HGRN_HEADS, HGRN_D, HGRN_D), F32)],
        compiler_params=_cparams(("parallel", "arbitrary")),
        name="hgrn2",
    )(q, lf, v, g, gnorm_w)


def _attn_kernel(q_ref, k0_ref, k1_ref, k2_ref, v0_ref, v1_ref, v2_ref, bias_ref, nw_ref, o_ref):
    i = pl.program_id(1)
    col = lax.broadcasted_iota(jnp.int32, (1, ATTN_KB), 1)
    pad_mask = jnp.where(col >= (2 - i) * ATTN_QB, 0.0, NEG_BIG)
    lane = lax.broadcasted_iota(jnp.int32, (1, LANES), 1)
    lo = lane < ATTN_HEAD_DIM
    for p in range(ATTN_HEADS // 2):
        cols = slice(p * LANES, (p + 1) * LANES)
        qp = q_ref[:, cols]
        kp = jnp.concatenate([k0_ref[:, cols], k1_ref[:, cols], k2_ref[:, cols]], axis=0)
        vp = jnp.concatenate([v0_ref[:, cols], v1_ref[:, cols], v2_ref[:, cols]], axis=0)
        outs = []
        for hh in range(2):
            qh = jnp.where(lo if hh == 0 else ~lo, qp, jnp.zeros_like(qp))
            s = _dot_nt(qh, kp) + bias_ref[2 * p + hh] + pad_mask
            e = jnp.exp(s - jnp.max(s, axis=-1, keepdims=True))
            denom = jnp.sum(e, axis=-1, keepdims=True)
            outs.append(jnp.dot(e.astype(BF16), vp, preferred_element_type=F32) / denom)
        o = jnp.where(lo, outs[0], outs[1])
        o2 = o * o
        ms0 = jnp.sum(jnp.where(lo, o2, 0.0), axis=-1, keepdims=True)
        ms1 = jnp.sum(jnp.where(lo, 0.0, o2), axis=-1, keepdims=True)
        ms = jnp.where(lo, ms0, ms1) * (1.0 / ATTN_HEAD_DIM)
        o_ref[:, cols] = (o * lax.rsqrt(ms + EPS) * nw_ref[:, cols]).astype(BF16)


def _attention(aq, ak, av, bias_tab, norm_w, batch, seq):
    nb = seq // ATTN_QB
    qblk = pl.BlockSpec((ATTN_QB, ATTN_WIDTH), lambda b, i: (b * nb + i, 0))

    def kblk(back):
        return pl.BlockSpec((ATTN_QB, ATTN_WIDTH),
                            lambda b, i: (b * nb + jnp.maximum(i - back, 0), 0))

    return pl.pallas_call(
        _attn_kernel,
        grid=(batch, nb),
        in_specs=[qblk, kblk(2), kblk(1), kblk(0), kblk(2), kblk(1), kblk(0),
                  pl.BlockSpec((ATTN_HEADS, ATTN_QB, ATTN_KB), lambda b, i: (0, 0, 0)),
                  pl.BlockSpec((1, ATTN_WIDTH), lambda b, i: (0, 0))],
        out_specs=qblk,
        out_shape=jax.ShapeDtypeStruct(aq.shape, BF16),
        compiler_params=_cparams(("parallel", "arbitrary")),
        name="band_attn",
    )(aq, ak, ak, ak, av, av, av, bias_tab, norm_w)


BIAS_ROW = ATTN_QB + ATTN_KB


def _bias_kernel(base_ref, o_ref):
    row = lax.broadcasted_iota(jnp.int32, (ATTN_QB, BIAS_ROW), 0)
    x = jnp.broadcast_to(base_ref[0], (ATTN_QB, BIAS_ROW))
    x = pltpu.roll(x, BIAS_ROW - ATTN_QB, 1)
    for bit in range(int(math.log2(ATTN_QB))):
        x = jnp.where(((row >> bit) & 1) == 1, pltpu.roll(x, 1 << bit, 1), x)
    r = lax.broadcasted_iota(jnp.int32, (ATTN_QB, ATTN_KB), 0)
    c = lax.broadcasted_iota(jnp.int32, (ATTN_QB, ATTN_KB), 1)
    dchunk = (2 * ATTN_QB) // CHUNK + r // CHUNK - c // CHUNK
    valid = (dchunk >= 0) & (dchunk <= LEFT_CHUNKS)
    o_ref[0] = jnp.where(valid, x[:, :ATTN_KB], NEG_BIG)


def _bias_table(rel_bias):
    H = rel_bias.shape[0]
    n_hi = 2 * ATTN_QB - MAX_REL + ATTN_QB
    n_lo = BIAS_ROW - n_hi - (2 * MAX_REL + 1)
    rb = rel_bias.astype(F32)
    base = jnp.concatenate([jnp.broadcast_to(rb[:, -1:], (H, n_hi)), rb[:, ::-1],
                            jnp.broadcast_to(rb[:, :1], (H, n_lo))], axis=1)
    return pl.pallas_call(
        _bias_kernel,
        grid=(H,),
        in_specs=[pl.BlockSpec((1, 1, BIAS_ROW), lambda h: (h, 0, 0))],
        out_specs=pl.BlockSpec((1, ATTN_QB, ATTN_KB), lambda h: (h, 0, 0)),
        out_shape=jax.ShapeDtypeStruct((H, ATTN_QB, ATTN_KB), F32),
        compiler_params=_cparams(("parallel",)),
        name="rel_bias_table",
    )(base.reshape(H, 1, BIAS_ROW))


def _outproj_router_kernel(yr_ref, ya_ref, x_ref, wo_ref, nw_ref, wr_ref,
                           x1_ref, h2_ref, route_ref, cnt_ref, run_ref):
    @pl.when(pl.program_id(0) == 0)
    def _():
        run_ref[...] = jnp.zeros_like(run_ref)

    acc = jnp.dot(yr_ref[...], wo_ref[0:HGRN_WIDTH, :], preferred_element_type=F32)
    acc = acc + jnp.dot(ya_ref[...], wo_ref[HGRN_WIDTH:, :], preferred_element_type=F32)
    x1 = x_ref[...] + acc
    x1_ref[...] = x1
    h = x1 * lax.rsqrt(jnp.mean(x1 * x1, axis=-1, keepdims=True) + EPS) * nw_ref[...]
    h_hi = h.astype(BF16)
    h2_ref[...] = h_hi.astype(F32)

    h_lo = (h - h_hi.astype(F32)).astype(BF16)
    hi_prod = jnp.dot(h_hi, wr_ref[...], preferred_element_type=F32)
    lg = (hi_prod[:, :LANES] + hi_prod[:, LANES:]
          + jnp.dot(h_lo, wr_ref[:, :LANES], preferred_element_type=F32))
    lane = lax.broadcasted_iota(jnp.int32, lg.shape, 1)
    lanef = lane.astype(F32)
    is_g = (lane >= N_EXPERTS) & (lane < N_EXPERTS + N_GROUPS)
    g_max = jnp.max(jnp.where(is_g, lg, -jnp.inf), axis=-1, keepdims=True)
    g_sum = jnp.sum(jnp.where(is_g, jnp.exp(lg - g_max), 0.0), axis=-1, keepdims=True)
    g_gate = 1.0 / g_sum
    g_idx = jnp.min(jnp.where(is_g & (lg == g_max), lanef - N_EXPERTS, 1e9),
                    axis=-1, keepdims=True)
    in_grp = (lane < N_EXPERTS) & ((lane // EXPERTS_PER_GROUP).astype(F32) == g_idx)
    v1 = jnp.max(jnp.where(in_grp, lg, -jnp.inf), axis=-1, keepdims=True)
    i1 = jnp.min(jnp.where(in_grp & (lg == v1), lanef, 1e9), axis=-1, keepdims=True)
    rest = in_grp & (lanef != i1)
    v2 = jnp.max(jnp.where(rest, lg, -jnp.inf), axis=-1, keepdims=True)
    i2 = jnp.min(jnp.where(rest & (lg == v2), lanef, 1e9), axis=-1, keepdims=True)
    e2 = jnp.exp(v2 - v1)
    gate1 = g_gate / (1.0 + e2)
    gate2 = gate1 * e2

    hit1 = lanef == i1
    hit2 = lanef == i2
    onehot = jnp.where(hit1 | hit2, 1.0, 0.0)
    n = lg.shape[0]
    rr = lax.broadcasted_iota(jnp.int32, (n, n), 0)
    cc = lax.broadcasted_iota(jnp.int32, (n, n), 1)
    earlier = jnp.where(cc < rr, 1.0, 0.0).astype(BF16)
    before = jnp.dot(earlier, onehot.astype(BF16), preferred_element_type=F32) + run_ref[...]
    rank1 = jnp.sum(jnp.where(hit1, before, 0.0), axis=-1, keepdims=True)
    rank2 = jnp.sum(jnp.where(hit2, before, 0.0), axis=-1, keepdims=True)
    run_ref[...] += jnp.sum(onehot, axis=0, keepdims=True)
    cnt_ref[...] = run_ref[...]

    rec = jnp.zeros_like(lg)
    for slot, val in ((R_E1, i1), (R_E2, i2), (R_RANK1, rank1), (R_RANK2, rank2),
                      (R_G1, gate1), (R_G2, gate2)):
        rec = jnp.where(lane == slot, val, rec)
    route_ref[...] = rec


def _outproj_router(y_rec, y_att, x2d, w_out_bf16, nw, w_router):
    T = x2d.shape[0]
    row = lambda i: (i, 0)
    const = lambda i: (0, 0)
    return pl.pallas_call(
        _outproj_router_kernel,
        grid=(T // TOK_TILE,),
        in_specs=[pl.BlockSpec((TOK_TILE, HGRN_WIDTH), row),
                  pl.BlockSpec((TOK_TILE, ATTN_WIDTH), row),
                  pl.BlockSpec((TOK_TILE, D_MODEL), row),
                  pl.BlockSpec((D_MODEL, D_MODEL), const),
                  pl.BlockSpec((1, D_MODEL), const),
                  pl.BlockSpec((D_MODEL, 2 * LANES), const)],
        out_specs=[pl.BlockSpec((TOK_TILE, D_MODEL), row),
                   pl.BlockSpec((TOK_TILE, D_MODEL), row),
                   pl.BlockSpec((TOK_TILE, LANES), row),
                   pl.BlockSpec((1, LANES), const)],
        out_shape=[jax.ShapeDtypeStruct((T, D_MODEL), F32),
                   jax.ShapeDtypeStruct((T, D_MODEL), F32),
                   jax.ShapeDtypeStruct((T, LANES), F32),
                   jax.ShapeDtypeStruct((1, LANES), F32)],
        scratch_shapes=[pltpu.VMEM((1, LANES), F32)],
        compiler_params=_cparams(("arbitrary",)),
        name="outproj_router",
    )(y_rec, y_att, x2d, w_out_bf16, nw, w_router)


def _row_copy(src_ref, src_row, dst_ref, dst_row, sem):
    return pltpu.make_async_copy(src_ref.at[pl.ds(src_row, 1), :],
                                 dst_ref.at[pl.ds(dst_row, 1), :], sem)


def _dispatch_kernel(pos1_ref, pos2_ref, h_ref, hs_init_ref, hs_ref, sem):
    del hs_init_ref

    def issue(t, carry):
        _row_copy(h_ref, t, hs_ref, pos1_ref[t], sem).start()
        _row_copy(h_ref, t, hs_ref, pos2_ref[t], sem).start()
        return carry

    lax.fori_loop(0, DISP_TILE, issue, 0, unroll=8)

    def drain(t, carry):
        _row_copy(h_ref, t, hs_ref, pos1_ref[t], sem).wait()
        _row_copy(h_ref, t, hs_ref, pos2_ref[t], sem).wait()
        return carry

    lax.fori_loop(0, DISP_TILE, drain, 0, unroll=8)


def _dispatch(pos1, pos2, h2, n_rows):
    T = h2.shape[0]
    smem_blk = pl.BlockSpec((DISP_TILE,), lambda i: (i,), memory_space=pltpu.SMEM)
    return pl.pallas_call(
        _dispatch_kernel,
        grid=(T // DISP_TILE,),
        in_specs=[smem_blk, smem_blk,
                  pl.BlockSpec((DISP_TILE, D_MODEL), lambda i: (i, 0)),
                  pl.BlockSpec(memory_space=pl.ANY)],
        out_specs=pl.BlockSpec(memory_space=pl.ANY),
        out_shape=jax.ShapeDtypeStruct((n_rows, D_MODEL), F32),
        scratch_shapes=[pltpu.SemaphoreType.DMA(())],
        input_output_aliases={3: 0},
        compiler_params=_cparams(("arbitrary",)),
        name="moe_dispatch",
    )(pos1, pos2, h2, jnp.zeros((n_rows, D_MODEL), F32))


def _expert_kernel(te_ref, nv_ref, hs_ref, w1_ref, w3_ref, w2_ref, ys_ref):
    j = pl.program_id(0)

    @pl.when(j < nv_ref[0])
    def _():
        h = hs_ref[...].astype(BF16)
        a = jnp.dot(h, w1_ref[0], preferred_element_type=F32)
        b = jnp.dot(h, w3_ref[0], preferred_element_type=F32)
        hid = (_silu(a) * b).astype(BF16)
        ys_ref[...] = jnp.dot(hid, w2_ref[0], preferred_element_type=F32)

    @pl.when(j >= nv_ref[0])
    def _():
        ys_ref[...] = jnp.zeros_like(ys_ref)


def _experts(tile_expert, n_valid, hs, w1, w3, w2):
    n_rows = hs.shape[0]
    rows_in = lambda j, te, nv: (jnp.minimum(j, nv[0] - 1), 0)
    wsel = lambda j, te, nv: (te[j], 0, 0)
    return pl.pallas_call(
        _expert_kernel,
        grid_spec=pltpu.PrefetchScalarGridSpec(
            num_scalar_prefetch=2,
            grid=(n_rows // MOE_ROWS,),
            in_specs=[pl.BlockSpec((MOE_ROWS, D_MODEL), rows_in),
                      pl.BlockSpec((1, D_MODEL, EXPERT_FF), wsel),
                      pl.BlockSpec((1, D_MODEL, EXPERT_FF), wsel),
                      pl.BlockSpec((1, EXPERT_FF, D_MODEL), wsel)],
            out_specs=pl.BlockSpec((MOE_ROWS, D_MODEL), lambda j, te, nv: (j, 0))),
        out_shape=jax.ShapeDtypeStruct((n_rows, D_MODEL), F32),
        compiler_params=_cparams(("arbitrary",)),
        name="moe_experts",
    )(tile_expert, n_valid, hs, w1, w3, w2)


def _combine_kernel(pos1_ref, pos2_ref, x1_ref, route_ref, fw_ref, ys_ref, o_ref,
                    buf_ref, sem, *, final_norm):
    i = pl.program_id(0)
    n = pl.num_programs(0)

    def gather(tile, slot, start):
        base = tile * COMB_TILE

        def body(t, carry):
            for k, pos_ref in enumerate((pos1_ref, pos2_ref)):
                cp = _row_copy(ys_ref, pos_ref[base + t], buf_ref.at[slot, k], t, sem.at[slot])
                if start:
                    cp.start()
                else:
                    cp.wait()
            return carry

        lax.fori_loop(0, COMB_TILE, body, 0, unroll=8)

    @pl.when(i == 0)
    def _():
        gather(0, 0, True)

    for slot in range(2):
        @pl.when((i + 1 < n) & ((i + 1) % 2 == slot))
        def _():
            gather(i + 1, slot, True)

    for slot in range(2):
        @pl.when(i % 2 == slot)
        def _():
            gather(i, slot, False)
            rec = route_ref[...]
            lane = lax.broadcasted_iota(jnp.int32, rec.shape, 1)
            g1 = jnp.sum(jnp.where(lane == R_G1, rec, 0.0), axis=-1, keepdims=True)
            g2 = jnp.sum(jnp.where(lane == R_G2, rec, 0.0), axis=-1, keepdims=True)
            x2 = x1_ref[...] + g1 * buf_ref[slot, 0] + g2 * buf_ref[slot, 1]
            if final_norm:
                x2 = x2 * lax.rsqrt(jnp.mean(x2 * x2, axis=-1, keepdims=True) + EPS) * fw_ref[...]
            o_ref[...] = x2


def _combine(pos1, pos2, x1, route, final_w, ys, final_norm):
    T = x1.shape[0]
    row = lambda i: (i, 0)
    return pl.pallas_call(
        functools.partial(_combine_kernel, final_norm=final_norm),
        grid=(T // COMB_TILE,),
        in_specs=[pl.BlockSpec(memory_space=pltpu.SMEM),
                  pl.BlockSpec(memory_space=pltpu.SMEM),
                  pl.BlockSpec((COMB_TILE, D_MODEL), row),
                  pl.BlockSpec((COMB_TILE, LANES), row),
                  pl.BlockSpec((1, D_MODEL), lambda i: (0, 0)),
                  pl.BlockSpec(memory_space=pl.ANY)],
        out_specs=pl.BlockSpec((COMB_TILE, D_MODEL), row),
        out_shape=jax.ShapeDtypeStruct((T, D_MODEL), F32),
        scratch_shapes=[pltpu.VMEM((2, TOP_K, COMB_TILE, D_MODEL), F32),
                        pltpu.SemaphoreType.DMA((2,))],
        compiler_params=_cparams(("arbitrary",)),
        name="moe_combine",
    )(pos1, pos2, x1, route, final_w, ys)


def _sorted_layout(route, counts, n_tiles):
    cnt = counts[0, :N_EXPERTS].astype(jnp.int32)
    padded = (cnt + MOE_ROWS - 1) // MOE_ROWS * MOE_ROWS
    ends = jnp.cumsum(padded)
    offs = ends - padded
    tile_start = jnp.arange(n_tiles, dtype=jnp.int32) * MOE_ROWS
    tile_expert = jnp.minimum(
        jnp.sum(ends[None, :] <= tile_start[:, None], axis=1), N_EXPERTS - 1).astype(jnp.int32)
    n_valid = (ends[-1:] // MOE_ROWS).astype(jnp.int32)
    ids = jnp.arange(N_EXPERTS, dtype=jnp.int32)

    def slot(e_lane, rank_lane):
        e = route[:, e_lane].astype(jnp.int32)
        seg = jnp.sum(jnp.where(e[:, None] == ids[None, :], offs[None, :], 0), axis=1)
        return seg + route[:, rank_lane].astype(jnp.int32)

    return slot(R_E1, R_RANK1), slot(R_E2, R_RANK2), tile_expert, n_valid


def kernel(x, norm_mix_w, w_in, hgrn_lower_bounds, hgrn_gnorm_w, attn_rel_bias, attn_norm_w,
           w_out, norm_ffn_w, moe_w_group, moe_w_expert, moe_w1, moe_w3, moe_w2, final_norm_w):
    B, S, D = x.shape
    T = B * S
    lb_all = jnp.cumsum(jax.nn.softmax(hgrn_lower_bounds.astype(F32), axis=0), axis=0)
    lb_all = lb_all - lb_all[0]
    log_lb = jnp.log(lb_all)
    log_1m_lb = jnp.log1p(-lb_all)

    xt = x.reshape(T, D)
    for l in range(DEPTH):
        q, lf, v, g, aq, ak, av = _norm_inproj(
            xt, norm_mix_w[l][None], w_in[l].astype(BF16), log_lb[l][None], log_1m_lb[l][None])
        y_rec = _hgrn(q, lf, v, g, hgrn_gnorm_w[l][None], B, S)
        y_att = _attention(aq, ak, av, _bias_table(attn_rel_bias[l]), attn_norm_w[l][None], B, S)
        w_r = jnp.concatenate(
            [moe_w_expert[l], moe_w_group[l],
             jnp.zeros((D, LANES - N_EXPERTS - N_GROUPS), F32)], axis=1)
        w_r_hi = w_r.astype(BF16)
        w_router = jnp.concatenate([w_r_hi, (w_r - w_r_hi.astype(F32)).astype(BF16)], axis=1)
        x1, h2, route, counts = _outproj_router(y_rec, y_att, xt, w_out[l].astype(BF16),
                                                norm_ffn_w[l][None], w_router)
        n_tiles = TOP_K * T // MOE_ROWS + N_EXPERTS
        pos1, pos2, tile_expert, n_valid = _sorted_layout(route, counts, n_tiles)
        hs = _dispatch(pos1, pos2, h2, n_tiles * MOE_ROWS)
        ys = _experts(tile_expert, n_valid, hs, moe_w1[l].astype(BF16), moe_w3[l].astype(BF16),
                      moe_w2[l].astype(BF16))
        xt = _combine(pos1, pos2, x1, route, final_norm_w[None], ys,
                      final_norm=(l == DEPTH - 1))
    return xt.reshape(B, S, D)
```

```python
import functools
import math

import jax
import jax.numpy as jnp
import numpy as np
from jax import lax
from jax.experimental import pallas as pl
from jax.experimental.pallas import tpu as pltpu

F32 = jnp.float32
BF16 = jnp.bfloat16

D_MODEL = 1024
DEPTH = 2
CHUNK = 64
HGRN_WIDTH = 512
HGRN_HEADS = 4
HGRN_D = 128
ATTN_WIDTH = 512
ATTN_HEADS = 8
ATTN_HEAD_DIM = 64
LEFT_CHUNKS = 8
MAX_REL = 128
N_GROUPS = 4
EXPERTS_PER_GROUP = 8
N_EXPERTS = 32
EXPERT_FF = 256
IN_COLS = 4 * HGRN_WIDTH + 3 * ATTN_WIDTH
EPS = 1e-6

LANES = 128
VMEM_LIMIT = 56 * 1024 * 1024
NEG_BIG = -1e30
LOG2_E = 1.4426950408889634

TOK_TILE = 512
ATTN_QB = 256
ATTN_KB = 3 * ATTN_QB
HGRN_TB = 512
TOP_K = 2
MOE_ROWS = 256
DISP_TILE = 1024
COMB_TILE = 512
R_E1, R_E2, R_RANK1, R_RANK2, R_G1, R_G2 = range(6)


def _cparams(sem):
    return pltpu.CompilerParams(dimension_semantics=sem, vmem_limit_bytes=VMEM_LIMIT)


def _silu(x):
    return x * jax.nn.sigmoid(x)


def _chunk_cumsum(x, row_in_chunk):
    acc = x
    step = 1
    while step < CHUNK:
        acc = acc + jnp.where(row_in_chunk >= step, pltpu.roll(acc, step, 0), 0.0)
        step *= 2
    return acc


def _norm_inproj_kernel(x_ref, nw_ref, w_ref, la_ref, lc_ref,
                        q_ref, lf_ref, bc_ref, v_ref, g_ref, aq_ref, ak_ref, av_ref, wbf_ref):
    @pl.when(pl.program_id(0) == 0)
    def _():
        wbf_ref[...] = w_ref[...].astype(BF16)

    x = x_ref[...]
    ms = jnp.mean(x * x, axis=-1, keepdims=True)
    h = (x * lax.rsqrt(ms + EPS) * nw_ref[...]).astype(BF16)

    def proj(j):
        return jnp.dot(h, wbf_ref[:, j * 512:(j + 1) * 512], preferred_element_type=F32)

    q_ref[...] = _silu(proj(0)).astype(BF16)
    z = proj(1)
    log_sig = jnp.minimum(z, 0.0) - jnp.log(1.0 + jnp.exp(-jnp.abs(z)))
    a = la_ref[...]
    b = lc_ref[...] + log_sig
    lf2 = (jnp.maximum(a, b) + jnp.log(1.0 + jnp.exp(-jnp.abs(a - b)))) * LOG2_E
    lf_ref[...] = lf2
    row_in_chunk = lax.broadcasted_iota(jnp.int32, lf2.shape, 0) & (CHUNK - 1)
    bc_ref[...] = _chunk_cumsum(lf2, row_in_chunk)
    v_ref[...] = proj(2).astype(BF16)
    g_ref[...] = _silu(proj(3)).astype(BF16)
    aq_ref[...] = (proj(4) * (LOG2_E / math.sqrt(ATTN_HEAD_DIM))).astype(BF16)
    ak_ref[...] = proj(5).astype(BF16)
    av_ref[...] = proj(6).astype(BF16)


def _norm_inproj(x2d, nw, w_in, layer, log_lb, log_1m_lb):
    T = x2d.shape[0]
    row = lambda i: (i, 0)
    const = lambda i: (0, 0)
    out_bf = jax.ShapeDtypeStruct((T, 512), BF16)
    out_f32 = jax.ShapeDtypeStruct((T, 512), F32)
    blk = pl.BlockSpec((TOK_TILE, 512), row)
    return pl.pallas_call(
        _norm_inproj_kernel,
        grid=(T // TOK_TILE,),
        in_specs=[pl.BlockSpec((TOK_TILE, D_MODEL), row),
                  pl.BlockSpec((1, D_MODEL), const),
                  pl.BlockSpec((None, D_MODEL, IN_COLS), lambda i: (layer, 0, 0),
                               pipeline_mode=pl.Buffered(1)),
                  pl.BlockSpec((1, 512), const),
                  pl.BlockSpec((1, 512), const)],
        out_specs=[blk] * 8,
        out_shape=[out_bf, out_f32, out_f32, out_bf, out_bf, out_bf, out_bf, out_bf],
        scratch_shapes=[pltpu.VMEM((D_MODEL, IN_COLS), BF16)],
        compiler_params=_cparams(("arbitrary",)),
        name="norm_inproj",
    )(x2d, nw, w_in, log_lb, log_1m_lb)


def _level_ref(bcum, half):
    if half >= 8:
        b3 = bcum.reshape(CHUNK // (2 * half), 2 * half, HGRN_D)
        ref = jnp.broadcast_to(b3[:, half - 1:half, :], b3.shape)
        return ref.reshape(CHUNK, HGRN_D)
    b3 = bcum.reshape(CHUNK // 8, 8, HGRN_D)
    if half == 4:
        ref = jnp.broadcast_to(b3[:, 3:4, :], b3.shape)
    else:
        sub = lax.broadcasted_iota(jnp.int32, b3.shape, 1)
        ref = jnp.where(sub < 4,
                        jnp.broadcast_to(b3[:, 1:2, :], b3.shape),
                        jnp.broadcast_to(b3[:, 5:6, :], b3.shape))
    return ref.reshape(CHUNK, HGRN_D)


def _dot_nt(a, b):
    return lax.dot_general(a, b, (((1,), (1,)), ((), ())), preferred_element_type=F32)


def _dot_tn(a, b):
    return lax.dot_general(a, b, (((0,), (0,)), ((), ())), preferred_element_type=F32)


HGRN_LEVELS = (32, 16, 8, 4, 2)


def _hgrn_kernel(q_ref, lf_ref, bc_ref, v_ref, g_ref, gw_ref, o_ref, *st_refs):
    @pl.when(pl.program_id(1) == 0)
    def _():
        for st_ref in st_refs:
            st_ref[...] = jnp.zeros_like(st_ref)

    row = lax.broadcasted_iota(jnp.int32, (CHUNK, HGRN_D), 0)
    sign = {half: jnp.where((row & half) != 0, 1.0, -1.0) for half in HGRN_LEVELS}
    r64 = lax.broadcasted_iota(jnp.int32, (CHUNK, CHUNK), 0)
    c64 = lax.broadcasted_iota(jnp.int32, (CHUNK, CHUNK), 1)
    differ = jnp.where(r64 > c64, r64 ^ c64, 0)
    level = sum((differ >= half).astype(jnp.int32) for half in (1,) + HGRN_LEVELS)

    def chunk_body(c, carry):
        rows = pl.ds(pl.multiple_of(c * CHUNK, CHUNK), CHUNK)
        heads = range(HGRN_HEADS)
        cols = [slice(h * HGRN_D, (h + 1) * HGRN_D) for h in heads]
        q = [q_ref[rows, cs].astype(F32) for cs in cols]
        f = [jnp.exp2(lf_ref[rows, cs]) for cs in cols]
        k = [1.0 - x for x in f]
        bcum = [bc_ref[rows, cs] for cs in cols]

        scores = [_dot_nt((q[h] * f[h]).astype(BF16), k[h].astype(BF16)) for h in heads]
        scores = [jnp.where(level == 1, s, 0.0) for s in scores]
        for half in HGRN_LEVELS:
            lvl = int(math.log2(half)) + 1
            for h in heads:
                fac = jnp.exp2((bcum[h] - _level_ref(bcum[h], half)) * sign[half])
                s_l = _dot_nt((q[h] * fac).astype(BF16), (k[h] * fac).astype(BF16))
                scores[h] = jnp.where(level == lvl, s_l, scores[h])

        v = [v_ref[rows, cs] for cs in cols]
        st = [st_ref[...] for st_ref in st_refs]
        inter = [_dot_nt((q[h] * jnp.exp2(bcum[h])).astype(BF16), st[h].astype(BF16))
                 for h in heads]
        intra = [jnp.dot(scores[h].astype(BF16), v[h], preferred_element_type=F32)
                 for h in heads]
        for h in heads:
            b_last = bcum[h][CHUNK - 1:CHUNK, :]
            k_dec = (k[h] * jnp.exp2(b_last - bcum[h])).astype(BF16)
            st_refs[h][...] = st[h] * jnp.exp2(b_last) + _dot_tn(v[h], k_dec)
        for h in heads:
            diag = jnp.sum(q[h] * k[h], axis=-1, keepdims=True) * v[h].astype(F32)
            o = intra[h] + diag + inter[h]
            o = o * lax.rsqrt(jnp.mean(o * o, axis=-1, keepdims=True) + EPS)
            o = o * gw_ref[:, cols[h]] * g_ref[rows, cols[h]].astype(F32)
            o_ref[rows, cols[h]] = o.astype(BF16)
        return carry

    lax.fori_loop(0, HGRN_TB // CHUNK, chunk_body, 0, unroll=2)


def _hgrn(q, lf, bc, v, g, gnorm_w, batch, seq):
    blk = pl.BlockSpec((HGRN_TB, HGRN_WIDTH), lambda b, i: (b * (seq // HGRN_TB) + i, 0))
    return pl.pallas_call(
        _hgrn_kernel,
        grid=(batch, seq // HGRN_TB),
        in_specs=[blk] * 5 + [pl.BlockSpec((1, HGRN_WIDTH), lambda b, i: (0, 0))],
        out_specs=blk,
        out_shape=jax.ShapeDtypeStruct(q.shape, BF16),
        scratch_shapes=[pltpu.VMEM((HGRN_D, HGRN_D), F32)] * HGRN_HEADS,
        compiler_params=_cparams(("parallel", "arbitrary")),
        name="hgrn2",
    )(q, lf, bc, v, g, gnorm_w)


def _attn_kernel(q_ref, k0_ref, k1_ref, k2_ref, v0_ref, v1_ref, v2_ref, bias_ref, nw_ref, o_ref):
    i = pl.program_id(1)
    lane = lax.broadcasted_iota(jnp.int32, (1, LANES), 1)
    lo = lane < ATTN_HEAD_DIM

    def block(pad_mask):
        def scores(head):
            cols = slice(head // 2 * LANES, (head // 2 + 1) * LANES)
            qp = q_ref[:, cols]
            kp = jnp.concatenate([k0_ref[:, cols], k1_ref[:, cols], k2_ref[:, cols]], axis=0)
            qh = jnp.where(lo if head % 2 == 0 else ~lo, qp, jnp.zeros_like(qp))
            return _dot_nt(qh, kp)

        s_next = scores(0)
        outs = []
        for head in range(ATTN_HEADS):
            s = s_next + bias_ref[head]
            if head + 1 < ATTN_HEADS:
                s_next = scores(head + 1)
            if pad_mask is not None:
                s = s + pad_mask
            e = jnp.exp2(s - jnp.max(s, axis=-1, keepdims=True))
            cols = slice(head // 2 * LANES, (head // 2 + 1) * LANES)
            vp = jnp.concatenate([v0_ref[:, cols], v1_ref[:, cols], v2_ref[:, cols]], axis=0)
            pv = jnp.dot(e.astype(BF16),
                         jnp.where(lo if head % 2 == 0 else ~lo, vp, jnp.ones_like(vp)),
                         preferred_element_type=F32)
            outs.append(pv / pltpu.roll(pv, ATTN_HEAD_DIM, 1))
            if head % 2 == 1:
                o = jnp.where(lo, outs[0], outs[1])
                outs = []
                o2 = o * o
                ms0 = jnp.sum(jnp.where(lo, o2, 0.0), axis=-1, keepdims=True)
                ms1 = jnp.sum(jnp.where(lo, 0.0, o2), axis=-1, keepdims=True)
                ms = jnp.where(lo, ms0, ms1) * (1.0 / ATTN_HEAD_DIM)
                o_ref[:, cols] = (o * lax.rsqrt(ms + EPS) * nw_ref[:, cols]).astype(BF16)

    @pl.when(i >= 2)
    def _():
        block(None)

    @pl.when(i < 2)
    def _():
        col = lax.broadcasted_iota(jnp.int32, (1, ATTN_KB), 1)
        block(jnp.where(col >= (2 - i) * ATTN_QB, 0.0, NEG_BIG))


def _attention(aq, ak, av, bias_tab, norm_w, batch, seq):
    nb = seq // ATTN_QB
    qblk = pl.BlockSpec((ATTN_QB, ATTN_WIDTH), lambda b, i: (b * nb + i, 0))

    def kblk(back):
        return pl.BlockSpec((ATTN_QB, ATTN_WIDTH),
                            lambda b, i: (b * nb + jnp.maximum(i - back, 0), 0))

    return pl.pallas_call(
        _attn_kernel,
        grid=(batch, nb),
        in_specs=[qblk, kblk(2), kblk(1), kblk(0), kblk(2), kblk(1), kblk(0),
                  pl.BlockSpec((ATTN_HEADS, ATTN_QB, ATTN_KB), lambda b, i: (0, 0, 0)),
                  pl.BlockSpec((1, ATTN_WIDTH), lambda b, i: (0, 0))],
        out_specs=qblk,
        out_shape=jax.ShapeDtypeStruct(aq.shape, BF16),
        compiler_params=_cparams(("parallel", "arbitrary")),
        name="band_attn",
    )(aq, ak, ak, ak, av, av, av, bias_tab, norm_w)


BIAS_ROW = ATTN_QB + ATTN_KB


def _bias_kernel(base_ref, o_ref):
    row = lax.broadcasted_iota(jnp.int32, (ATTN_QB, BIAS_ROW), 0)
    x = jnp.broadcast_to(base_ref[0], (ATTN_QB, BIAS_ROW))
    x = pltpu.roll(x, BIAS_ROW - ATTN_QB, 1)
    for bit in range(int(math.log2(ATTN_QB))):
        x = jnp.where(((row >> bit) & 1) == 1, pltpu.roll(x, 1 << bit, 1), x)
    r = lax.broadcasted_iota(jnp.int32, (ATTN_QB, ATTN_KB), 0)
    c = lax.broadcasted_iota(jnp.int32, (ATTN_QB, ATTN_KB), 1)
    dchunk = (2 * ATTN_QB) // CHUNK + r // CHUNK - c // CHUNK
    valid = (dchunk >= 0) & (dchunk <= LEFT_CHUNKS)
    o_ref[0] = jnp.where(valid, x[:, :ATTN_KB] * LOG2_E, NEG_BIG)


def _bias_table(rel_bias):
    H = rel_bias.shape[0]
    n_hi = 2 * ATTN_QB - MAX_REL + ATTN_QB
    n_lo = BIAS_ROW - n_hi - (2 * MAX_REL + 1)
    rb = rel_bias.astype(F32)
    base = jnp.concatenate([jnp.broadcast_to(rb[:, -1:], (H, n_hi)), rb[:, ::-1],
                            jnp.broadcast_to(rb[:, :1], (H, n_lo))], axis=1)
    return pl.pallas_call(
        _bias_kernel,
        grid=(H,),
        in_specs=[pl.BlockSpec((1, 1, BIAS_ROW), lambda h: (h, 0, 0))],
        out_specs=pl.BlockSpec((1, ATTN_QB, ATTN_KB), lambda h: (h, 0, 0)),
        out_shape=jax.ShapeDtypeStruct((H, ATTN_QB, ATTN_KB), F32),
        compiler_params=_cparams(("parallel",)),
        name="rel_bias_table",
    )(base.reshape(H, 1, BIAS_ROW))


def _outproj_router_kernel(yr_ref, ya_ref, x_ref, wo_ref, nw_ref, wr_ref,
                           x1_ref, h2_ref, route_ref, cnt_ref, run_ref, wob_ref):
    @pl.when(pl.program_id(0) == 0)
    def _():
        run_ref[...] = jnp.zeros_like(run_ref)
        wob_ref[...] = wo_ref[...].astype(BF16)

    acc = jnp.dot(yr_ref[...], wob_ref[0:HGRN_WIDTH, :], preferred_element_type=F32)
    acc = acc + jnp.dot(ya_ref[...], wob_ref[HGRN_WIDTH:, :], preferred_element_type=F32)
    x1 = x_ref[...] + acc
    x1_ref[...] = x1
    h = x1 * lax.rsqrt(jnp.mean(x1 * x1, axis=-1, keepdims=True) + EPS) * nw_ref[...]
    h_hi = h.astype(BF16)
    h2_ref[...] = h_hi.astype(F32)

    h_lo = (h - h_hi.astype(F32)).astype(BF16)
    hi_prod = jnp.dot(h_hi, wr_ref[...], preferred_element_type=F32)
    lg = (hi_prod[:, :LANES] + hi_prod[:, LANES:]
          + jnp.dot(h_lo, wr_ref[:, :LANES], preferred_element_type=F32))
    lane = lax.broadcasted_iota(jnp.int32, lg.shape, 1)
    lanef = lane.astype(F32)
    is_g = (lane >= N_EXPERTS) & (lane < N_EXPERTS + N_GROUPS)
    g_max = jnp.max(jnp.where(is_g, lg, -jnp.inf), axis=-1, keepdims=True)
    g_sum = jnp.sum(jnp.where(is_g, jnp.exp(lg - g_max), 0.0), axis=-1, keepdims=True)
    g_gate = 1.0 / g_sum
    g_idx = jnp.min(jnp.where(is_g & (lg == g_max), lanef - N_EXPERTS, 1e9),
                    axis=-1, keepdims=True)
    in_grp = (lane < N_EXPERTS) & ((lane // EXPERTS_PER_GROUP).astype(F32) == g_idx)
    v1 = jnp.max(jnp.where(in_grp, lg, -jnp.inf), axis=-1, keepdims=True)
    i1 = jnp.min(jnp.where(in_grp & (lg == v1), lanef, 1e9), axis=-1, keepdims=True)
    rest = in_grp & (lanef != i1)
    v2 = jnp.max(jnp.where(rest, lg, -jnp.inf), axis=-1, keepdims=True)
    i2 = jnp.min(jnp.where(rest & (lg == v2), lanef, 1e9), axis=-1, keepdims=True)
    e2 = jnp.exp(v2 - v1)
    gate1 = g_gate / (1.0 + e2)
    gate2 = gate1 * e2

    hit1 = lanef == i1
    hit2 = lanef == i2
    onehot = jnp.where(hit1 | hit2, 1.0, 0.0)
    n = lg.shape[0]
    rr = lax.broadcasted_iota(jnp.int32, (n, n), 0)
    cc = lax.broadcasted_iota(jnp.int32, (n, n), 1)
    earlier = jnp.where(cc < rr, 1.0, 0.0).astype(BF16)
    before = jnp.dot(earlier, onehot.astype(BF16), preferred_element_type=F32) + run_ref[...]
    rank1 = jnp.sum(jnp.where(hit1, before, 0.0), axis=-1, keepdims=True)
    rank2 = jnp.sum(jnp.where(hit2, before, 0.0), axis=-1, keepdims=True)
    run_ref[...] += jnp.sum(onehot, axis=0, keepdims=True)
    cnt_ref[...] = run_ref[...]

    rec = jnp.zeros_like(lg)
    for slot, val in ((R_E1, i1), (R_E2, i2), (R_RANK1, rank1), (R_RANK2, rank2),
                      (R_G1, gate1), (R_G2, gate2)):
        rec = jnp.where(lane == slot, val, rec)
    route_ref[...] = rec


def _outproj_router(y_rec, y_att, x2d, w_out, layer, nw, w_router):
    T = x2d.shape[0]
    row = lambda i: (i, 0)
    const = lambda i: (0, 0)
    return pl.pallas_call(
        _outproj_router_kernel,
        grid=(T // TOK_TILE,),
        in_specs=[pl.BlockSpec((TOK_TILE, HGRN_WIDTH), row),
                  pl.BlockSpec((TOK_TILE, ATTN_WIDTH), row),
                  pl.BlockSpec((TOK_TILE, D_MODEL), row),
                  pl.BlockSpec((None, D_MODEL, D_MODEL), lambda i: (layer, 0, 0)),
                  pl.BlockSpec((1, D_MODEL), const),
                  pl.BlockSpec((D_MODEL, 2 * LANES), const)],
        out_specs=[pl.BlockSpec((TOK_TILE, D_MODEL), row),
                   pl.BlockSpec((TOK_TILE, D_MODEL), row),
                   pl.BlockSpec((TOK_TILE, LANES), row),
                   pl.BlockSpec((1, LANES), const)],
        out_shape=[jax.ShapeDtypeStruct((T, D_MODEL), F32),
                   jax.ShapeDtypeStruct((T, D_MODEL), F32),
                   jax.ShapeDtypeStruct((T, LANES), F32),
                   jax.ShapeDtypeStruct((1, LANES), F32)],
        scratch_shapes=[pltpu.VMEM((1, LANES), F32), pltpu.VMEM((D_MODEL, D_MODEL), BF16)],
        compiler_params=_cparams(("arbitrary",)),
        name="outproj_router",
    )(y_rec, y_att, x2d, w_out, nw, w_router)


def _row_copy(src_ref, src_row, dst_ref, dst_row, sem):
    return pltpu.make_async_copy(src_ref.at[pl.ds(src_row, 1), :],
                                 dst_ref.at[pl.ds(dst_row, 1), :], sem)


def _dispatch_kernel(pos1_ref, pos2_ref, h_ref, hs_init_ref, hs_ref, sem):
    del hs_init_ref

    def issue(t, carry):
        _row_copy(h_ref, t, hs_ref, pos1_ref[t], sem).start()
        _row_copy(h_ref, t, hs_ref, pos2_ref[t], sem).start()
        return carry

    lax.fori_loop(0, DISP_TILE, issue, 0, unroll=8)

    def drain(t, carry):
        _row_copy(h_ref, t, hs_ref, pos1_ref[t], sem).wait()
        _row_copy(h_ref, t, hs_ref, pos2_ref[t], sem).wait()
        return carry

    lax.fori_loop(0, DISP_TILE, drain, 0, unroll=8)


def _dispatch(pos1, pos2, h2, n_rows):
    T = h2.shape[0]
    smem_blk = pl.BlockSpec((DISP_TILE,), lambda i: (i,), memory_space=pltpu.SMEM)
    return pl.pallas_call(
        _dispatch_kernel,
        grid=(T // DISP_TILE,),
        in_specs=[smem_blk, smem_blk,
                  pl.BlockSpec((DISP_TILE, D_MODEL), lambda i: (i, 0)),
                  pl.BlockSpec(memory_space=pl.ANY)],
        out_specs=pl.BlockSpec(memory_space=pl.ANY),
        out_shape=jax.ShapeDtypeStruct((n_rows, D_MODEL), F32),
        scratch_shapes=[pltpu.SemaphoreType.DMA(())],
        input_output_aliases={3: 0},
        compiler_params=_cparams(("arbitrary",)),
        name="moe_dispatch",
    )(pos1, pos2, h2, jnp.zeros((n_rows, D_MODEL), F32))


def _expert_kernel(te_ref, nv_ref, hs_ref, w1_ref, w3_ref, w2_ref, ys_ref,
                   w1b_ref, w3b_ref, w2b_ref):
    j = pl.program_id(0)
    valid = j < nv_ref[0]
    new_expert = (j == 0) | (te_ref[j] != te_ref[jnp.maximum(j - 1, 0)])

    @pl.when(valid & new_expert)
    def _():
        w1b_ref[...] = w1_ref[...].astype(BF16)
        w3b_ref[...] = w3_ref[...].astype(BF16)
        w2b_ref[...] = w2_ref[...].astype(BF16)

    @pl.when(valid)
    def _():
        h = hs_ref[...].astype(BF16)
        a = jnp.dot(h, w1b_ref[...], preferred_element_type=F32)
        b = jnp.dot(h, w3b_ref[...], preferred_element_type=F32)
        hid = (_silu(a) * b).astype(BF16)
        ys_ref[...] = jnp.dot(hid, w2b_ref[...], preferred_element_type=F32)

    @pl.when(jnp.logical_not(valid))
    def _():
        ys_ref[...] = jnp.zeros_like(ys_ref)


def _experts(tile_expert, n_valid, hs, w1, w3, w2, layer):
    n_rows = hs.shape[0]
    rows_in = lambda j, te, nv: (jnp.minimum(j, nv[0] - 1), 0)
    wsel = lambda j, te, nv: (layer, te[j], 0, 0)
    return pl.pallas_call(
        _expert_kernel,
        grid_spec=pltpu.PrefetchScalarGridSpec(
            num_scalar_prefetch=2,
            grid=(n_rows // MOE_ROWS,),
            in_specs=[pl.BlockSpec((MOE_ROWS, D_MODEL), rows_in),
                      pl.BlockSpec((None, None, D_MODEL, EXPERT_FF), wsel),
                      pl.BlockSpec((None, None, D_MODEL, EXPERT_FF), wsel),
                      pl.BlockSpec((None, None, EXPERT_FF, D_MODEL), wsel)],
            out_specs=pl.BlockSpec((MOE_ROWS, D_MODEL), lambda j, te, nv: (j, 0)),
            scratch_shapes=[pltpu.VMEM((D_MODEL, EXPERT_FF), BF16),
                            pltpu.VMEM((D_MODEL, EXPERT_FF), BF16),
                            pltpu.VMEM((EXPERT_FF, D_MODEL), BF16)]),
        out_shape=jax.ShapeDtypeStruct((n_rows, D_MODEL), F32),
        compiler_params=_cparams(("arbitrary",)),
        name="moe_experts",
    )(tile_expert, n_valid, hs, w1, w3, w2)


def _combine_kernel(pos1_ref, pos2_ref, x1_ref, route_ref, fw_ref, ys_ref, o_ref,
                    buf_ref, sem, *, final_norm):
    i = pl.program_id(0)
    n = pl.num_programs(0)

    def gather(tile, slot, start):
        base = tile * COMB_TILE

        def body(t, carry):
            for k, pos_ref in enumerate((pos1_ref, pos2_ref)):
                cp = _row_copy(ys_ref, pos_ref[base + t], buf_ref.at[slot, k], t, sem.at[slot])
                if start:
                    cp.start()
                else:
                    cp.wait()
            return carry

        lax.fori_loop(0, COMB_TILE, body, 0, unroll=8)

    @pl.when(i == 0)
    def _():
        gather(0, 0, True)

    for slot in range(2):
        @pl.when((i + 1 < n) & ((i + 1) % 2 == slot))
        def _():
            gather(i + 1, slot, True)

    for slot in range(2):
        @pl.when(i % 2 == slot)
        def _():
            gather(i, slot, False)
            rec = route_ref[...]
            lane = lax.broadcasted_iota(jnp.int32, rec.shape, 1)
            g1 = jnp.sum(jnp.where(lane == R_G1, rec, 0.0), axis=-1, keepdims=True)
            g2 = jnp.sum(jnp.where(lane == R_G2, rec, 0.0), axis=-1, keepdims=True)
            x2 = x1_ref[...] + g1 * buf_ref[slot, 0] + g2 * buf_ref[slot, 1]
            if final_norm:
                x2 = x2 * lax.rsqrt(jnp.mean(x2 * x2, axis=-1, keepdims=True) + EPS) * fw_ref[...]
            o_ref[...] = x2


def _combine(pos1, pos2, x1, route, final_w, ys, final_norm):
    T = x1.shape[0]
    row = lambda i: (i, 0)
    return pl.pallas_call(
        functools.partial(_combine_kernel, final_norm=final_norm),
        grid=(T // COMB_TILE,),
        in_specs=[pl.BlockSpec(memory_space=pltpu.SMEM),
                  pl.BlockSpec(memory_space=pltpu.SMEM),
                  pl.BlockSpec((COMB_TILE, D_MODEL), row),
                  pl.BlockSpec((COMB_TILE, LANES), row),
                  pl.BlockSpec((1, D_MODEL), lambda i: (0, 0)),
                  pl.BlockSpec(memory_space=pl.ANY)],
        out_specs=pl.BlockSpec((COMB_TILE, D_MODEL), row),
        out_shape=jax.ShapeDtypeStruct((T, D_MODEL), F32),
        scratch_shapes=[pltpu.VMEM((2, TOP_K, COMB_TILE, D_MODEL), F32),
                        pltpu.SemaphoreType.DMA((2,))],
        compiler_params=_cparams(("arbitrary",)),
        name="moe_combine",
    )(pos1, pos2, x1, route, final_w, ys)


def _sorted_layout(route, counts, n_tiles):
    cnt = counts[0, :N_EXPERTS].astype(jnp.int32)
    padded = (cnt + MOE_ROWS - 1) // MOE_ROWS * MOE_ROWS
    ends = jnp.cumsum(padded)
    offs = ends - padded
    tile_start = jnp.arange(n_tiles, dtype=jnp.int32) * MOE_ROWS
    tile_expert = jnp.minimum(
        jnp.sum(ends[None, :] <= tile_start[:, None], axis=1), N_EXPERTS - 1).astype(jnp.int32)
    n_valid = (ends[-1:] // MOE_ROWS).astype(jnp.int32)
    ids = jnp.arange(N_EXPERTS, dtype=jnp.int32)

    def slot(e_lane, rank_lane):
        e = route[:, e_lane].astype(jnp.int32)
        seg = jnp.sum(jnp.where(e[:, None] == ids[None, :], offs[None, :], 0), axis=1)
        return seg + route[:, rank_lane].astype(jnp.int32)

    return slot(R_E1, R_RANK1), slot(R_E2, R_RANK2), tile_expert, n_valid


def kernel(x, norm_mix_w, w_in, hgrn_lower_bounds, hgrn_gnorm_w, attn_rel_bias, attn_norm_w,
           w_out, norm_ffn_w, moe_w_group, moe_w_expert, moe_w1, moe_w3, moe_w2, final_norm_w):
    B, S, D = x.shape
    T = B * S
    lb_all = jnp.cumsum(jax.nn.softmax(hgrn_lower_bounds.astype(F32), axis=0), axis=0)
    lb_all = lb_all - lb_all[0]
    log_lb = jnp.log(lb_all)
    log_1m_lb = jnp.log1p(-lb_all)

    xt = x.reshape(T, D)
    for l in range(DEPTH):
        q, lf, bc, v, g, aq, ak, av = _norm_inproj(
            xt, norm_mix_w[l][None], w_in, l, log_lb[l][None], log_1m_lb[l][None])
        y_rec = _hgrn(q, lf, bc, v, g, hgrn_gnorm_w[l][None], B, S)
        y_att = _attention(aq, ak, av, _bias_table(attn_rel_bias[l]), attn_norm_w[l][None], B, S)
        w_r = jnp.concatenate(
            [moe_w_expert[l], moe_w_group[l],
             jnp.zeros((D, LANES - N_EXPERTS - N_GROUPS), F32)], axis=1)
        w_r_hi = w_r.astype(BF16)
        w_router = jnp.concatenate([w_r_hi, (w_r - w_r_hi.astype(F32)).astype(BF16)], axis=1)
        x1, h2, route, counts = _outproj_router(y_rec, y_att, xt, w_out, l,
                                                norm_ffn_w[l][None], w_router)
        n_tiles = TOP_K * T // MOE_ROWS + N_EXPERTS
        pos1, pos2, tile_expert, n_valid = _sorted_layout(route, counts, n_tiles)
        hs = _dispatch(pos1, pos2, h2, n_tiles * MOE_ROWS)
        ys = _experts(tile_expert, n_valid, hs, moe_w1, moe_w3, moe_w2, l)
        xt = _combine(pos1, pos2, x1, route, final_norm_w[None], ys,
                      final_norm=(l == DEPTH - 1))
    return xt.reshape(B, S, D)
```

```python
import functools
import math

import jax
import jax.numpy as jnp
import numpy as np
from jax import lax
from jax.experimental import pallas as pl
from jax.experimental.pallas import tpu as pltpu

F32 = jnp.float32
BF16 = jnp.bfloat16

D_MODEL = 1024
DEPTH = 2
CHUNK = 64
HGRN_WIDTH = 512
HGRN_HEADS = 4
HGRN_D = 128
ATTN_WIDTH = 512
ATTN_HEADS = 8
ATTN_HEAD_DIM = 64
LEFT_CHUNKS = 8
MAX_REL = 128
N_GROUPS = 4
EXPERTS_PER_GROUP = 8
N_EXPERTS = 32
EXPERT_FF = 256
IN_COLS = 4 * HGRN_WIDTH + 3 * ATTN_WIDTH
EPS = 1e-6

LANES = 128
VMEM_LIMIT = 56 * 1024 * 1024
NEG_BIG = -1e30
LOG2_E = 1.4426950408889634

TOK_TILE = 512
ATTN_QB = 256
ATTN_KB = 3 * ATTN_QB
HGRN_TB = 512
MOE_ROWS = 1024
MOE_EPS = 4
ROW_W = D_MODEL + LANES
DISP_TILE = 1024
COMB_TILE = 512
R_E1, R_E2, R_G1, R_G2, R_GROUP, R_RANK = range(6)


def _cparams(sem):
    return pltpu.CompilerParams(dimension_semantics=sem, vmem_limit_bytes=VMEM_LIMIT)


def _silu(x):
    return x * jax.nn.sigmoid(x)


def _chunk_cumsum(x, row_in_chunk):
    acc = x
    step = 1
    while step < CHUNK:
        acc = acc + jnp.where(row_in_chunk >= step, pltpu.roll(acc, step, 0), 0.0)
        step *= 2
    return acc


def _norm_inproj_kernel(x_ref, nw_ref, w_ref, la_ref, lc_ref,
                        q_ref, lf_ref, bc_ref, v_ref, g_ref, aq_ref, ak_ref, av_ref, wbf_ref):
    @pl.when(pl.program_id(0) == 0)
    def _():
        wbf_ref[...] = w_ref[...].astype(BF16)

    x = x_ref[...]
    ms = jnp.mean(x * x, axis=-1, keepdims=True)
    h = (x * lax.rsqrt(ms + EPS) * nw_ref[...]).astype(BF16)

    def proj(j):
        return jnp.dot(h, wbf_ref[:, j * 512:(j + 1) * 512], preferred_element_type=F32)

    q_ref[...] = _silu(proj(0)).astype(BF16)
    z = proj(1)
    log_sig = jnp.minimum(z, 0.0) - jnp.log(1.0 + jnp.exp(-jnp.abs(z)))
    a = la_ref[...]
    b = lc_ref[...] + log_sig
    lf2 = (jnp.maximum(a, b) + jnp.log(1.0 + jnp.exp(-jnp.abs(a - b)))) * LOG2_E
    lf_ref[...] = lf2
    row_in_chunk = lax.broadcasted_iota(jnp.int32, lf2.shape, 0) & (CHUNK - 1)
    bc_ref[...] = _chunk_cumsum(lf2, row_in_chunk)
    v_ref[...] = proj(2).astype(BF16)
    g_ref[...] = _silu(proj(3)).astype(BF16)
    aq_ref[...] = (proj(4) * (LOG2_E / math.sqrt(ATTN_HEAD_DIM))).astype(BF16)
    ak_ref[...] = proj(5).astype(BF16)
    av_ref[...] = proj(6).astype(BF16)


def _norm_inproj(x2d, nw, w_in, layer, log_lb, log_1m_lb):
    T = x2d.shape[0]
    row = lambda i: (i, 0)
    const = lambda i: (0, 0)
    out_bf = jax.ShapeDtypeStruct((T, 512), BF16)
    out_f32 = jax.ShapeDtypeStruct((T, 512), F32)
    blk = pl.BlockSpec((TOK_TILE, 512), row)
    return pl.pallas_call(
        _norm_inproj_kernel,
        grid=(T // TOK_TILE,),
        in_specs=[pl.BlockSpec((TOK_TILE, D_MODEL), row),
                  pl.BlockSpec((1, D_MODEL), const),
                  pl.BlockSpec((None, D_MODEL, IN_COLS), lambda i: (layer, 0, 0),
                               pipeline_mode=pl.Buffered(1)),
                  pl.BlockSpec((1, 512), const),
                  pl.BlockSpec((1, 512), const)],
        out_specs=[blk] * 8,
        out_shape=[out_bf, out_f32, out_f32, out_bf, out_bf, out_bf, out_bf, out_bf],
        scratch_shapes=[pltpu.VMEM((D_MODEL, IN_COLS), BF16)],
        compiler_params=_cparams(("arbitrary",)),
        name="norm_inproj",
    )(x2d, nw, w_in, log_lb, log_1m_lb)


def _level_ref(bcum, half):
    if half >= 8:
        b3 = bcum.reshape(CHUNK // (2 * half), 2 * half, HGRN_D)
        ref = jnp.broadcast_to(b3[:, half - 1:half, :], b3.shape)
        return ref.reshape(CHUNK, HGRN_D)
    b3 = bcum.reshape(CHUNK // 8, 8, HGRN_D)
    if half == 4:
        ref = jnp.broadcast_to(b3[:, 3:4, :], b3.shape)
    else:
        sub = lax.broadcasted_iota(jnp.int32, b3.shape, 1)
        ref = jnp.where(sub < 4,
                        jnp.broadcast_to(b3[:, 1:2, :], b3.shape),
                        jnp.broadcast_to(b3[:, 5:6, :], b3.shape))
    return ref.reshape(CHUNK, HGRN_D)


def _dot_nt(a, b):
    return lax.dot_general(a, b, (((1,), (1,)), ((), ())), preferred_element_type=F32)


def _dot_tn(a, b):
    return lax.dot_general(a, b, (((0,), (0,)), ((), ())), preferred_element_type=F32)


HGRN_LEVELS = (32, 16, 8, 4, 2)


def _hgrn_kernel(q_ref, lf_ref, bc_ref, v_ref, g_ref, gw_ref, o_ref, *st_refs):
    @pl.when(pl.program_id(1) == 0)
    def _():
        for st_ref in st_refs:
            st_ref[...] = jnp.zeros_like(st_ref)

    row = lax.broadcasted_iota(jnp.int32, (CHUNK, HGRN_D), 0)
    sign = {half: jnp.where((row & half) != 0, 1.0, -1.0) for half in HGRN_LEVELS}
    r64 = lax.broadcasted_iota(jnp.int32, (CHUNK, CHUNK), 0)
    c64 = lax.broadcasted_iota(jnp.int32, (CHUNK, CHUNK), 1)
    differ = jnp.where(r64 > c64, r64 ^ c64, 0)
    level = sum((differ >= half).astype(jnp.int32) for half in (1,) + HGRN_LEVELS)

    def chunk_body(c, carry):
        rows = pl.ds(pl.multiple_of(c * CHUNK, CHUNK), CHUNK)
        heads = range(HGRN_HEADS)
        cols = [slice(h * HGRN_D, (h + 1) * HGRN_D) for h in heads]
        q = [q_ref[rows, cs].astype(F32) for cs in cols]
        f = [jnp.exp2(lf_ref[rows, cs]) for cs in cols]
        k = [1.0 - x for x in f]
        bcum = [bc_ref[rows, cs] for cs in cols]

        scores = [_dot_nt((q[h] * f[h]).astype(BF16), k[h].astype(BF16)) for h in heads]
        scores = [jnp.where(level == 1, s, 0.0) for s in scores]
        for half in HGRN_LEVELS:
            lvl = int(math.log2(half)) + 1
            for h in heads:
                fac = jnp.exp2((bcum[h] - _level_ref(bcum[h], half)) * sign[half])
                s_l = _dot_nt((q[h] * fac).astype(BF16), (k[h] * fac).astype(BF16))
                scores[h] = jnp.where(level == lvl, s_l, scores[h])

        v = [v_ref[rows, cs] for cs in cols]
        st = [st_ref[...] for st_ref in st_refs]
        inter = [_dot_nt((q[h] * jnp.exp2(bcum[h])).astype(BF16), st[h].astype(BF16))
                 for h in heads]
        intra = [jnp.dot(scores[h].astype(BF16), v[h], preferred_element_type=F32)
                 for h in heads]
        for h in heads:
            b_last = bcum[h][CHUNK - 1:CHUNK, :]
            k_dec = (k[h] * jnp.exp2(b_last - bcum[h])).astype(BF16)
            st_refs[h][...] = st[h] * jnp.exp2(b_last) + _dot_tn(v[h], k_dec)
        for h in heads:
            diag = jnp.sum(q[h] * k[h], axis=-1, keepdims=True) * v[h].astype(F32)
            o = intra[h] + diag + inter[h]
            o = o * lax.rsqrt(jnp.mean(o * o, axis=-1, keepdims=True) + EPS)
            o = o * gw_ref[:, cols[h]] * g_ref[rows, cols[h]].astype(F32)
            o_ref[rows, cols[h]] = o.astype(BF16)
        return carry

    lax.fori_loop(0, HGRN_TB // CHUNK, chunk_body, 0, unroll=2)


def _hgrn(q, lf, bc, v, g, gnorm_w, batch, seq):
    blk = pl.BlockSpec((HGRN_TB, HGRN_WIDTH), lambda b, i: (b * (seq // HGRN_TB) + i, 0))
    return pl.pallas_call(
        _hgrn_kernel,
        grid=(batch, seq // HGRN_TB),
        in_specs=[blk] * 5 + [pl.BlockSpec((1, HGRN_WIDTH), lambda b, i: (0, 0))],
        out_specs=blk,
        out_shape=jax.ShapeDtypeStruct(q.shape, BF16),
        scratch_shapes=[pltpu.VMEM((HGRN_D, HGRN_D), F32)] * HGRN_HEADS,
        compiler_params=_cparams(("parallel", "arbitrary")),
        name="hgrn2",
    )(q, lf, bc, v, g, gnorm_w)


def _attn_kernel(q_ref, k0_ref, k1_ref, k2_ref, v0_ref, v1_ref, v2_ref, bias_ref, nw_ref, o_ref):
    i = pl.program_id(1)
    lane = lax.broadcasted_iota(jnp.int32, (1, LANES), 1)
    lo = lane < ATTN_HEAD_DIM

    def block(pad_mask):
        def scores(head):
            cols = slice(head // 2 * LANES, (head // 2 + 1) * LANES)
            qp = q_ref[:, cols]
            kp = jnp.concatenate([k0_ref[:, cols], k1_ref[:, cols], k2_ref[:, cols]], axis=0)
            qh = jnp.where(lo if head % 2 == 0 else ~lo, qp, jnp.zeros_like(qp))
            return _dot_nt(qh, kp)

        s_next = scores(0)
        outs = []
        for head in range(ATTN_HEADS):
            s = s_next + bias_ref[head]
            if head + 1 < ATTN_HEADS:
                s_next = scores(head + 1)
            if pad_mask is not None:
                s = s + pad_mask
            e = jnp.exp2(s - jnp.max(s, axis=-1, keepdims=True))
            cols = slice(head // 2 * LANES, (head // 2 + 1) * LANES)
            vp = jnp.concatenate([v0_ref[:, cols], v1_ref[:, cols], v2_ref[:, cols]], axis=0)
            pv = jnp.dot(e.astype(BF16),
                         jnp.where(lo if head % 2 == 0 else ~lo, vp, jnp.ones_like(vp)),
                         preferred_element_type=F32)
            outs.append(pv / pltpu.roll(pv, ATTN_HEAD_DIM, 1))
            if head % 2 == 1:
                o = jnp.where(lo, outs[0], outs[1])
                outs = []
                o2 = o * o
                ms0 = jnp.sum(jnp.where(lo, o2, 0.0), axis=-1, keepdims=True)
                ms1 = jnp.sum(jnp.where(lo, 0.0, o2), axis=-1, keepdims=True)
                ms = jnp.where(lo, ms0, ms1) * (1.0 / ATTN_HEAD_DIM)
                o_ref[:, cols] = (o * lax.rsqrt(ms + EPS) * nw_ref[:, cols]).astype(BF16)

    @pl.when(i >= 2)
    def _():
        block(None)

    @pl.when(i < 2)
    def _():
        col = lax.broadcasted_iota(jnp.int32, (1, ATTN_KB), 1)
        block(jnp.where(col >= (2 - i) * ATTN_QB, 0.0, NEG_BIG))


def _attention(aq, ak, av, bias_tab, norm_w, batch, seq):
    nb = seq // ATTN_QB
    qblk = pl.BlockSpec((ATTN_QB, ATTN_WIDTH), lambda b, i: (b * nb + i, 0))

    def kblk(back):
        return pl.BlockSpec((ATTN_QB, ATTN_WIDTH),
                            lambda b, i: (b * nb + jnp.maximum(i - back, 0), 0))

    return pl.pallas_call(
        _attn_kernel,
        grid=(batch, nb),
        in_specs=[qblk, kblk(2), kblk(1), kblk(0), kblk(2), kblk(1), kblk(0),
                  pl.BlockSpec((ATTN_HEADS, ATTN_QB, ATTN_KB), lambda b, i: (0, 0, 0)),
                  pl.BlockSpec((1, ATTN_WIDTH), lambda b, i: (0, 0))],
        out_specs=qblk,
        out_shape=jax.ShapeDtypeStruct(aq.shape, BF16),
        compiler_params=_cparams(("parallel", "arbitrary")),
        name="band_attn",
    )(aq, ak, ak, ak, av, av, av, bias_tab, norm_w)


BIAS_ROW = ATTN_QB + ATTN_KB


def _bias_kernel(base_ref, o_ref):
    row = lax.broadcasted_iota(jnp.int32, (ATTN_QB, BIAS_ROW), 0)
    x = jnp.broadcast_to(base_ref[0], (ATTN_QB, BIAS_ROW))
    x = pltpu.roll(x, BIAS_ROW - ATTN_QB, 1)
    for bit in range(int(math.log2(ATTN_QB))):
        x = jnp.where(((row >> bit) & 1) == 1, pltpu.roll(x, 1 << bit, 1), x)
    r = lax.broadcasted_iota(jnp.int32, (ATTN_QB, ATTN_KB), 0)
    c = lax.broadcasted_iota(jnp.int32, (ATTN_QB, ATTN_KB), 1)
    dchunk = (2 * ATTN_QB) // CHUNK + r // CHUNK - c // CHUNK
    valid = (dchunk >= 0) & (dchunk <= LEFT_CHUNKS)
    o_ref[0] = jnp.where(valid, x[:, :ATTN_KB] * LOG2_E, NEG_BIG)


def _bias_table(rel_bias):
    H = rel_bias.shape[0]
    n_hi = 2 * ATTN_QB - MAX_REL + ATTN_QB
    n_lo = BIAS_ROW - n_hi - (2 * MAX_REL + 1)
    rb = rel_bias.astype(F32)
    base = jnp.concatenate([jnp.broadcast_to(rb[:, -1:], (H, n_hi)), rb[:, ::-1],
                            jnp.broadcast_to(rb[:, :1], (H, n_lo))], axis=1)
    return pl.pallas_call(
        _bias_kernel,
        grid=(H,),
        in_specs=[pl.BlockSpec((1, 1, BIAS_ROW), lambda h: (h, 0, 0))],
        out_specs=pl.BlockSpec((1, ATTN_QB, ATTN_KB), lambda h: (h, 0, 0)),
        out_shape=jax.ShapeDtypeStruct((H, ATTN_QB, ATTN_KB), F32),
        compiler_params=_cparams(("parallel",)),
        name="rel_bias_table",
    )(base.reshape(H, 1, BIAS_ROW))


def _outproj_router_kernel(yr_ref, ya_ref, x_ref, wo_ref, nw_ref, wr_ref,
                           x1_ref, h2_ref, route_ref, cnt_ref, run_ref, wob_ref):
    @pl.when(pl.program_id(0) == 0)
    def _():
        run_ref[...] = jnp.zeros_like(run_ref)
        wob_ref[...] = wo_ref[...].astype(BF16)

    acc = jnp.dot(yr_ref[...], wob_ref[0:HGRN_WIDTH, :], preferred_element_type=F32)
    acc = acc + jnp.dot(ya_ref[...], wob_ref[HGRN_WIDTH:, :], preferred_element_type=F32)
    x1 = x_ref[...] + acc
    x1_ref[...] = x1
    h = x1 * lax.rsqrt(jnp.mean(x1 * x1, axis=-1, keepdims=True) + EPS) * nw_ref[...]
    h_hi = h.astype(BF16)
    h2_ref[:, :D_MODEL] = h_hi.astype(F32)

    h_lo = (h - h_hi.astype(F32)).astype(BF16)
    hi_prod = jnp.dot(h_hi, wr_ref[...], preferred_element_type=F32)
    lg = (hi_prod[:, :LANES] + hi_prod[:, LANES:]
          + jnp.dot(h_lo, wr_ref[:, :LANES], preferred_element_type=F32))
    lane = lax.broadcasted_iota(jnp.int32, lg.shape, 1)
    lanef = lane.astype(F32)
    is_g = (lane >= N_EXPERTS) & (lane < N_EXPERTS + N_GROUPS)
    g_max = jnp.max(jnp.where(is_g, lg, -jnp.inf), axis=-1, keepdims=True)
    g_sum = jnp.sum(jnp.where(is_g, jnp.exp(lg - g_max), 0.0), axis=-1, keepdims=True)
    g_gate = 1.0 / g_sum
    g_idx = jnp.min(jnp.where(is_g & (lg == g_max), lanef - N_EXPERTS, 1e9),
                    axis=-1, keepdims=True)
    in_grp = (lane < N_EXPERTS) & ((lane // EXPERTS_PER_GROUP).astype(F32) == g_idx)
    v1 = jnp.max(jnp.where(in_grp, lg, -jnp.inf), axis=-1, keepdims=True)
    i1 = jnp.min(jnp.where(in_grp & (lg == v1), lanef, 1e9), axis=-1, keepdims=True)
    rest = in_grp & (lanef != i1)
    v2 = jnp.max(jnp.where(rest, lg, -jnp.inf), axis=-1, keepdims=True)
    i2 = jnp.min(jnp.where(rest & (lg == v2), lanef, 1e9), axis=-1, keepdims=True)
    e2 = jnp.exp(v2 - v1)
    gate1 = g_gate / (1.0 + e2)
    gate2 = gate1 * e2

    hit = lanef == g_idx
    onehot = jnp.where(hit, 1.0, 0.0)
    n = lg.shape[0]
    rr = lax.broadcasted_iota(jnp.int32, (n, n), 0)
    cc = lax.broadcasted_iota(jnp.int32, (n, n), 1)
    earlier = jnp.where(cc < rr, 1.0, 0.0).astype(BF16)
    before = jnp.dot(earlier, onehot.astype(BF16), preferred_element_type=F32) + run_ref[...]
    rank = jnp.sum(jnp.where(hit, before, 0.0), axis=-1, keepdims=True)
    run_ref[...] += jnp.sum(onehot, axis=0, keepdims=True)
    cnt_ref[...] = run_ref[...]

    rec = jnp.zeros_like(lg)
    for slot, val in ((R_E1, i1), (R_E2, i2), (R_G1, gate1), (R_G2, gate2),
                      (R_GROUP, g_idx), (R_RANK, rank)):
        rec = jnp.where(lane == slot, val, rec)
    route_ref[...] = rec
    h2_ref[:, D_MODEL:] = rec


def _outproj_router(y_rec, y_att, x2d, w_out, layer, nw, w_router):
    T = x2d.shape[0]
    row = lambda i: (i, 0)
    const = lambda i: (0, 0)
    return pl.pallas_call(
        _outproj_router_kernel,
        grid=(T // TOK_TILE,),
        in_specs=[pl.BlockSpec((TOK_TILE, HGRN_WIDTH), row),
                  pl.BlockSpec((TOK_TILE, ATTN_WIDTH), row),
                  pl.BlockSpec((TOK_TILE, D_MODEL), row),
                  pl.BlockSpec((None, D_MODEL, D_MODEL), lambda i: (layer, 0, 0)),
                  pl.BlockSpec((1, D_MODEL), const),
                  pl.BlockSpec((D_MODEL, 2 * LANES), const)],
        out_specs=[pl.BlockSpec((TOK_TILE, D_MODEL), row),
                   pl.BlockSpec((TOK_TILE, ROW_W), row),
                   pl.BlockSpec((TOK_TILE, LANES), row),
                   pl.BlockSpec((1, LANES), const)],
        out_shape=[jax.ShapeDtypeStruct((T, D_MODEL), F32),
                   jax.ShapeDtypeStruct((T, ROW_W), F32),
                   jax.ShapeDtypeStruct((T, LANES), F32),
                   jax.ShapeDtypeStruct((1, LANES), F32)],
        scratch_shapes=[pltpu.VMEM((1, LANES), F32), pltpu.VMEM((D_MODEL, D_MODEL), BF16)],
        compiler_params=_cparams(("arbitrary",)),
        name="outproj_router",
    )(y_rec, y_att, x2d, w_out, nw, w_router)


def _row_copy(src_ref, src_row, dst_ref, dst_row, sem):
    return pltpu.make_async_copy(src_ref.at[pl.ds(src_row, 1), :],
                                 dst_ref.at[pl.ds(dst_row, 1), :], sem)


def _dispatch_kernel(pos_ref, h_ref, hs_init_ref, hs_ref, sem):
    del hs_init_ref

    def issue(t, carry):
        _row_copy(h_ref, t, hs_ref, pos_ref[t], sem).start()
        return carry

    lax.fori_loop(0, DISP_TILE, issue, 0, unroll=8)

    def drain(t, carry):
        _row_copy(h_ref, t, hs_ref, pos_ref[t], sem).wait()
        return carry

    lax.fori_loop(0, DISP_TILE, drain, 0, unroll=8)


def _dispatch(pos, h2, n_rows):
    T = h2.shape[0]
    return pl.pallas_call(
        _dispatch_kernel,
        grid=(T // DISP_TILE,),
        in_specs=[pl.BlockSpec((DISP_TILE,), lambda i: (i,), memory_space=pltpu.SMEM),
                  pl.BlockSpec((DISP_TILE, ROW_W), lambda i: (i, 0)),
                  pl.BlockSpec(memory_space=pl.ANY)],
        out_specs=pl.BlockSpec(memory_space=pl.ANY),
        out_shape=jax.ShapeDtypeStruct((n_rows, ROW_W), F32),
        scratch_shapes=[pltpu.SemaphoreType.DMA(())],
        input_output_aliases={2: 0},
        compiler_params=_cparams(("arbitrary",)),
        name="moe_dispatch",
    )(pos, h2, jnp.zeros((n_rows, ROW_W), F32))


def _expert_kernel(tg_ref, nv_ref, hs_ref, w1_ref, w3_ref, w2_ref, ys_ref, hb_ref, cw_ref):
    j = pl.program_id(0)
    e = pl.program_id(1)
    valid = j < nv_ref[0]

    @pl.when(valid & (e == 0))
    def _():
        hb_ref[...] = hs_ref[:, :D_MODEL].astype(BF16)
        rec = hs_ref[:, D_MODEL:]
        lane = lax.broadcasted_iota(jnp.int32, rec.shape, 1)

        def field(slot):
            return jnp.sum(jnp.where(lane == slot, rec, 0.0), axis=-1, keepdims=True)

        first = (tg_ref[j] * EXPERTS_PER_GROUP).astype(F32)
        lanef = lane.astype(F32)
        cw_ref[...] = (jnp.where(field(R_E1) - first == lanef, field(R_G1), 0.0)
                       + jnp.where(field(R_E2) - first == lanef, field(R_G2), 0.0))

    @pl.when(valid)
    def _():
        h = hb_ref[...]
        cw = cw_ref[...]
        lane = lax.broadcasted_iota(jnp.int32, cw.shape, 1)
        hid = []
        for k in range(MOE_EPS):
            gate = jnp.sum(jnp.where(lane == e * MOE_EPS + k, cw, 0.0), axis=-1, keepdims=True)
            a = jnp.dot(h, w1_ref[k].astype(BF16), preferred_element_type=F32)
            b = jnp.dot(h, w3_ref[k].astype(BF16), preferred_element_type=F32)
            hid.append((_silu(a) * b * gate).astype(BF16))
        w2 = w2_ref[...].astype(BF16).reshape(MOE_EPS * EXPERT_FF, D_MODEL)
        y = jnp.dot(jnp.concatenate(hid, axis=-1), w2, preferred_element_type=F32)

        @pl.when(e == 0)
        def _():
            ys_ref[...] = y

        @pl.when(e > 0)
        def _():
            ys_ref[...] += y

    @pl.when(jnp.logical_not(valid) & (e == 0))
    def _():
        ys_ref[...] = jnp.zeros_like(ys_ref)


def _experts(tile_group, n_valid, hs, w1, w3, w2, layer):
    n_rows = hs.shape[0]
    rows_in = lambda j, e, tg, nv: (jnp.minimum(j, nv[0] - 1), 0)
    steps = EXPERTS_PER_GROUP // MOE_EPS

    def wsel(j, e, tg, nv):
        last = nv[0] - 1
        blk = jnp.where(j <= last, tg[j] * steps + e, tg[last] * steps + steps - 1)
        return (layer, blk, 0, 0)

    return pl.pallas_call(
        _expert_kernel,
        grid_spec=pltpu.PrefetchScalarGridSpec(
            num_scalar_prefetch=2,
            grid=(n_rows // MOE_ROWS, steps),
            in_specs=[pl.BlockSpec((MOE_ROWS, ROW_W), rows_in),
                      pl.BlockSpec((None, MOE_EPS, D_MODEL, EXPERT_FF), wsel),
                      pl.BlockSpec((None, MOE_EPS, D_MODEL, EXPERT_FF), wsel),
                      pl.BlockSpec((None, MOE_EPS, EXPERT_FF, D_MODEL), wsel)],
            out_specs=pl.BlockSpec((MOE_ROWS, D_MODEL), lambda j, e, tg, nv: (j, 0)),
            scratch_shapes=[pltpu.VMEM((MOE_ROWS, D_MODEL), BF16),
                            pltpu.VMEM((MOE_ROWS, LANES), F32)]),
        out_shape=jax.ShapeDtypeStruct((n_rows, D_MODEL), F32),
        compiler_params=_cparams(("arbitrary", "arbitrary")),
        name="moe_experts",
    )(tile_group, n_valid, hs, w1, w3, w2)


def _combine_kernel(pos_ref, x1_ref, fw_ref, ys_ref, o_ref, buf_ref, sem, *, final_norm):
    i = pl.program_id(0)
    n = pl.num_programs(0)

    def gather(tile, slot, start):
        base = tile * COMB_TILE

        def body(t, carry):
            cp = _row_copy(ys_ref, pos_ref[base + t], buf_ref.at[slot], t, sem.at[slot])
            if start:
                cp.start()
            else:
                cp.wait()
            return carry

        lax.fori_loop(0, COMB_TILE, body, 0, unroll=8)

    @pl.when(i == 0)
    def _():
        gather(0, 0, True)

    for slot in range(2):
        @pl.when((i + 1 < n) & ((i + 1) % 2 == slot))
        def _():
            gather(i + 1, slot, True)

    for slot in range(2):
        @pl.when(i % 2 == slot)
        def _():
            gather(i, slot, False)
            x2 = x1_ref[...] + buf_ref[slot]
            if final_norm:
                x2 = x2 * lax.rsqrt(jnp.mean(x2 * x2, axis=-1, keepdims=True) + EPS) * fw_ref[...]
            o_ref[...] = x2


def _combine(pos, x1, final_w, ys, final_norm):
    T = x1.shape[0]
    row = lambda i: (i, 0)
    return pl.pallas_call(
        functools.partial(_combine_kernel, final_norm=final_norm),
        grid=(T // COMB_TILE,),
        in_specs=[pl.BlockSpec(memory_space=pltpu.SMEM),
                  pl.BlockSpec((COMB_TILE, D_MODEL), row),
                  pl.BlockSpec((1, D_MODEL), lambda i: (0, 0)),
                  pl.BlockSpec(memory_space=pl.ANY)],
        out_specs=pl.BlockSpec((COMB_TILE, D_MODEL), row),
        out_shape=jax.ShapeDtypeStruct((T, D_MODEL), F32),
        scratch_shapes=[pltpu.VMEM((2, COMB_TILE, D_MODEL), F32),
                        pltpu.SemaphoreType.DMA((2,))],
        compiler_params=_cparams(("arbitrary",)),
        name="moe_combine",
    )(pos, x1, final_w, ys)


def _sorted_layout(route, counts, n_tiles):
    cnt = counts[0, :N_GROUPS].astype(jnp.int32)
    padded = (cnt + MOE_ROWS - 1) // MOE_ROWS * MOE_ROWS
    ends = jnp.cumsum(padded)
    offs = ends - padded
    tile_start = jnp.arange(n_tiles, dtype=jnp.int32) * MOE_ROWS
    tile_group = jnp.minimum(
        jnp.sum(ends[None, :] <= tile_start[:, None], axis=1), N_GROUPS - 1).astype(jnp.int32)
    n_valid = (ends[-1:] // MOE_ROWS).astype(jnp.int32)
    group = route[:, R_GROUP].astype(jnp.int32)
    ids = jnp.arange(N_GROUPS, dtype=jnp.int32)
    seg = jnp.sum(jnp.where(group[:, None] == ids[None, :], offs[None, :], 0), axis=1)
    return seg + route[:, R_RANK].astype(jnp.int32), tile_group, n_valid


def kernel(x, norm_mix_w, w_in, hgrn_lower_bounds, hgrn_gnorm_w, attn_rel_bias, attn_norm_w,
           w_out, norm_ffn_w, moe_w_group, moe_w_expert, moe_w1, moe_w3, moe_w2, final_norm_w):
    B, S, D = x.shape
    T = B * S
    lb_all = jnp.cumsum(jax.nn.softmax(hgrn_lower_bounds.astype(F32), axis=0), axis=0)
    lb_all = lb_all - lb_all[0]
    log_lb = jnp.log(lb_all)
    log_1m_lb = jnp.log1p(-lb_all)

    xt = x.reshape(T, D)
    for l in range(DEPTH):
        q, lf, bc, v, g, aq, ak, av = _norm_inproj(
            xt, norm_mix_w[l][None], w_in, l, log_lb[l][None], log_1m_lb[l][None])
        y_rec = _hgrn(q, lf, bc, v, g, hgrn_gnorm_w[l][None], B, S)
        y_att = _attention(aq, ak, av, _bias_table(attn_rel_bias[l]), attn_norm_w[l][None], B, S)
        w_r = jnp.concatenate(
            [moe_w_expert[l], moe_w_group[l],
             jnp.zeros((D, LANES - N_EXPERTS - N_GROUPS), F32)], axis=1)
        w_r_hi = w_r.astype(BF16)
        w_router = jnp.concatenate([w_r_hi, (w_r - w_r_hi.astype(F32)).astype(BF16)], axis=1)
        x1, h2, route, counts = _outproj_router(y_rec, y_att, xt, w_out, l,
                                                norm_ffn_w[l][None], w_router)
        n_tiles = T // MOE_ROWS + N_GROUPS
        pos, tile_group, n_valid = _sorted_layout(route, counts, n_tiles)
        hs = _dispatch(pos, h2, n_tiles * MOE_ROWS)
        ys = _experts(tile_group, n_valid, hs, moe_w1, moe_w3, moe_w2, l)
        xt = _combine(pos, x1, final_norm_w[None], ys, final_norm=(l == DEPTH - 1))
    return xt.reshape(B, S, D)
```

```python
import functools
import math

import jax
import jax.numpy as jnp
import numpy as np
from jax import lax
from jax.experimental import pallas as pl
from jax.experimental.pallas import tpu as pltpu

F32 = jnp.float32
BF16 = jnp.bfloat16

D_MODEL = 1024
DEPTH = 2
CHUNK = 64
HGRN_WIDTH = 512
HGRN_HEADS = 4
HGRN_D = 128
ATTN_WIDTH = 512
ATTN_HEADS = 8
ATTN_HEAD_DIM = 64
LEFT_CHUNKS = 8
MAX_REL = 128
N_GROUPS = 4
EXPERTS_PER_GROUP = 8
N_EXPERTS = 32
EXPERT_FF = 256
IN_COLS = 4 * HGRN_WIDTH + 3 * ATTN_WIDTH
EPS = 1e-6

LANES = 128
VMEM_LIMIT = 56 * 1024 * 1024
NEG_BIG = -1e30
LOG2_E = 1.4426950408889634

TOK_TILE = 512
ATTN_QB = 256
ATTN_KB = 3 * ATTN_QB
HGRN_TB = 512
MOE_ROWS = 1024
MOE_EPS = 4
ROW_W = D_MODEL + LANES
DISP_TILE = 1024
COMB_TILE = 512
R_E1, R_E2, R_G1, R_G2, R_GROUP, R_RANK = range(6)


def _cparams(sem):
    return pltpu.CompilerParams(dimension_semantics=sem, vmem_limit_bytes=VMEM_LIMIT)


def _silu(x):
    return x * jax.nn.sigmoid(x)


def _chunk_cumsum(x, row_in_chunk):
    acc = x
    step = 1
    while step < CHUNK:
        acc = acc + jnp.where(row_in_chunk >= step, pltpu.roll(acc, step, 0), 0.0)
        step *= 2
    return acc


def _norm_inproj_kernel(x_ref, nw_ref, w_ref, la_ref, lc_ref,
                        q_ref, lf_ref, bc_ref, v_ref, g_ref, aq_ref, ak_ref, av_ref, wbf_ref):
    @pl.when(pl.program_id(0) == 0)
    def _():
        wbf_ref[...] = w_ref[...].astype(BF16)

    x = x_ref[...]
    ms = jnp.mean(x * x, axis=-1, keepdims=True)
    h = (x * lax.rsqrt(ms + EPS) * nw_ref[...]).astype(BF16)

    def proj(j):
        return jnp.dot(h, wbf_ref[:, j * 512:(j + 1) * 512], preferred_element_type=F32)

    q_ref[...] = _silu(proj(0)).astype(BF16)
    z = proj(1)
    log_sig = jnp.minimum(z, 0.0) - jnp.log(1.0 + jnp.exp(-jnp.abs(z)))
    a = la_ref[...]
    b = lc_ref[...] + log_sig
    lf2 = (jnp.maximum(a, b) + jnp.log(1.0 + jnp.exp(-jnp.abs(a - b)))) * LOG2_E
    lf_ref[...] = lf2
    row_in_chunk = lax.broadcasted_iota(jnp.int32, lf2.shape, 0) & (CHUNK - 1)
    bc_ref[...] = _chunk_cumsum(lf2, row_in_chunk)
    v_ref[...] = proj(2).astype(BF16)
    g_ref[...] = _silu(proj(3)).astype(BF16)
    aq_ref[...] = (proj(4) * (LOG2_E / math.sqrt(ATTN_HEAD_DIM))).astype(BF16)
    ak_ref[...] = proj(5).astype(BF16)
    av_ref[...] = proj(6).astype(BF16)


def _norm_inproj(x2d, nw, w_in, layer, log_lb, log_1m_lb):
    T = x2d.shape[0]
    row = lambda i: (i, 0)
    const = lambda i: (0, 0)
    out_bf = jax.ShapeDtypeStruct((T, 512), BF16)
    out_f32 = jax.ShapeDtypeStruct((T, 512), F32)
    blk = pl.BlockSpec((TOK_TILE, 512), row)
    return pl.pallas_call(
        _norm_inproj_kernel,
        grid=(T // TOK_TILE,),
        in_specs=[pl.BlockSpec((TOK_TILE, D_MODEL), row),
                  pl.BlockSpec((1, D_MODEL), const),
                  pl.BlockSpec((None, D_MODEL, IN_COLS), lambda i: (layer, 0, 0),
                               pipeline_mode=pl.Buffered(1)),
                  pl.BlockSpec((1, 512), const),
                  pl.BlockSpec((1, 512), const)],
        out_specs=[blk] * 8,
        out_shape=[out_bf, out_f32, out_f32, out_bf, out_bf, out_bf, out_bf, out_bf],
        scratch_shapes=[pltpu.VMEM((D_MODEL, IN_COLS), BF16)],
        compiler_params=_cparams(("arbitrary",)),
        name="norm_inproj",
    )(x2d, nw, w_in, log_lb, log_1m_lb)


def _level_ref(bcum, half):
    if half >= 8:
        b3 = bcum.reshape(CHUNK // (2 * half), 2 * half, HGRN_D)
        ref = jnp.broadcast_to(b3[:, half - 1:half, :], b3.shape)
        return ref.reshape(CHUNK, HGRN_D)
    b3 = bcum.reshape(CHUNK // 8, 8, HGRN_D)
    if half == 4:
        ref = jnp.broadcast_to(b3[:, 3:4, :], b3.shape)
    else:
        sub = lax.broadcasted_iota(jnp.int32, b3.shape, 1)
        ref = jnp.where(sub < 4,
                        jnp.broadcast_to(b3[:, 1:2, :], b3.shape),
                        jnp.broadcast_to(b3[:, 5:6, :], b3.shape))
    return ref.reshape(CHUNK, HGRN_D)


def _dot_nt(a, b):
    return lax.dot_general(a, b, (((1,), (1,)), ((), ())), preferred_element_type=F32)


def _dot_tn(a, b):
    return lax.dot_general(a, b, (((0,), (0,)), ((), ())), preferred_element_type=F32)


HGRN_LEVELS = (32, 16, 8, 4, 2)


def _hgrn_kernel(q_ref, lf_ref, bc_ref, v_ref, g_ref, gw_ref, o_ref, *st_refs):
    @pl.when(pl.program_id(1) == 0)
    def _():
        for st_ref in st_refs:
            st_ref[...] = jnp.zeros_like(st_ref)

    row = lax.broadcasted_iota(jnp.int32, (CHUNK, HGRN_D), 0)
    sign = {half: jnp.where((row & half) != 0, 1.0, -1.0) for half in HGRN_LEVELS}
    r64 = lax.broadcasted_iota(jnp.int32, (CHUNK, CHUNK), 0)
    c64 = lax.broadcasted_iota(jnp.int32, (CHUNK, CHUNK), 1)
    differ = jnp.where(r64 > c64, r64 ^ c64, 0)
    level = sum((differ >= half).astype(jnp.int32) for half in (1,) + HGRN_LEVELS)

    def chunk_body(c, carry):
        rows = pl.ds(pl.multiple_of(c * CHUNK, CHUNK), CHUNK)
        heads = range(HGRN_HEADS)
        cols = [slice(h * HGRN_D, (h + 1) * HGRN_D) for h in heads]
        q = [q_ref[rows, cs].astype(F32) for cs in cols]
        f = [jnp.exp2(lf_ref[rows, cs]) for cs in cols]
        k = [1.0 - x for x in f]
        bcum = [bc_ref[rows, cs] for cs in cols]

        scores = [_dot_nt((q[h] * f[h]).astype(BF16), k[h].astype(BF16)) for h in heads]
        scores = [jnp.where(level == 1, s, 0.0) for s in scores]
        for half in HGRN_LEVELS:
            lvl = int(math.log2(half)) + 1
            for h in heads:
                fac = jnp.exp2((bcum[h] - _level_ref(bcum[h], half)) * sign[half])
                s_l = _dot_nt((q[h] * fac).astype(BF16), (k[h] * fac).astype(BF16))
                scores[h] = jnp.where(level == lvl, s_l, scores[h])

        v = [v_ref[rows, cs] for cs in cols]
        st = [st_ref[...] for st_ref in st_refs]
        inter = [_dot_nt((q[h] * jnp.exp2(bcum[h])).astype(BF16), st[h].astype(BF16))
                 for h in heads]
        intra = [jnp.dot(scores[h].astype(BF16), v[h], preferred_element_type=F32)
                 for h in heads]
        for h in heads:
            b_last = bcum[h][CHUNK - 1:CHUNK, :]
            k_dec = (k[h] * jnp.exp2(b_last - bcum[h])).astype(BF16)
            st_refs[h][...] = st[h] * jnp.exp2(b_last) + _dot_tn(v[h], k_dec)
        for h in heads:
            diag = jnp.sum(q[h] * k[h], axis=-1, keepdims=True) * v[h].astype(F32)
            o = intra[h] + diag + inter[h]
            o = o * lax.rsqrt(jnp.mean(o * o, axis=-1, keepdims=True) + EPS)
            o = o * gw_ref[:, cols[h]] * g_ref[rows, cols[h]].astype(F32)
            o_ref[rows, cols[h]] = o.astype(BF16)
        return carry

    lax.fori_loop(0, HGRN_TB // CHUNK, chunk_body, 0, unroll=2)


def _hgrn(q, lf, bc, v, g, gnorm_w, batch, seq):
    blk = pl.BlockSpec((HGRN_TB, HGRN_WIDTH), lambda b, i: (b * (seq // HGRN_TB) + i, 0))
    return pl.pallas_call(
        _hgrn_kernel,
        grid=(batch, seq // HGRN_TB),
        in_specs=[blk] * 5 + [pl.BlockSpec((1, HGRN_WIDTH), lambda b, i: (0, 0))],
        out_specs=blk,
        out_shape=jax.ShapeDtypeStruct(q.shape, BF16),
        scratch_shapes=[pltpu.VMEM((HGRN_D, HGRN_D), F32)] * HGRN_HEADS,
        compiler_params=_cparams(("parallel", "arbitrary")),
        name="hgrn2",
    )(q, lf, bc, v, g, gnorm_w)


def _attn_kernel(q_ref, k0_ref, k1_ref, k2_ref, v0_ref, v1_ref, v2_ref, bias_ref, nw_ref, o_ref):
    i = pl.program_id(1)
    lane = lax.broadcasted_iota(jnp.int32, (1, LANES), 1)
    lo = lane < ATTN_HEAD_DIM

    def block(pad_mask):
        def scores(head):
            cols = slice(head // 2 * LANES, (head // 2 + 1) * LANES)
            qp = q_ref[:, cols]
            kp = jnp.concatenate([k0_ref[:, cols], k1_ref[:, cols], k2_ref[:, cols]], axis=0)
            qh = jnp.where(lo if head % 2 == 0 else ~lo, qp, jnp.zeros_like(qp))
            return _dot_nt(qh, kp)

        s_next = scores(0)
        outs = []
        for head in range(ATTN_HEADS):
            s = s_next + bias_ref[head]
            if head + 1 < ATTN_HEADS:
                s_next = scores(head + 1)
            if pad_mask is not None:
                s = s + pad_mask
            e = jnp.exp2(s - jnp.max(s, axis=-1, keepdims=True))
            cols = slice(head // 2 * LANES, (head // 2 + 1) * LANES)
            vp = jnp.concatenate([v0_ref[:, cols], v1_ref[:, cols], v2_ref[:, cols]], axis=0)
            pv = jnp.dot(e.astype(BF16),
                         jnp.where(lo if head % 2 == 0 else ~lo, vp, jnp.ones_like(vp)),
                         preferred_element_type=F32)
            outs.append(pv / pltpu.roll(pv, ATTN_HEAD_DIM, 1))
            if head % 2 == 1:
                o = jnp.where(lo, outs[0], outs[1])
                outs = []
                o2 = o * o
                ms0 = jnp.sum(jnp.where(lo, o2, 0.0), axis=-1, keepdims=True)
                ms1 = jnp.sum(jnp.where(lo, 0.0, o2), axis=-1, keepdims=True)
                ms = jnp.where(lo, ms0, ms1) * (1.0 / ATTN_HEAD_DIM)
                o_ref[:, cols] = (o * lax.rsqrt(ms + EPS) * nw_ref[:, cols]).astype(BF16)

    @pl.when(i >= 2)
    def _():
        block(None)

    @pl.when(i < 2)
    def _():
        col = lax.broadcasted_iota(jnp.int32, (1, ATTN_KB), 1)
        block(jnp.where(col >= (2 - i) * ATTN_QB, 0.0, NEG_BIG))


def _attention(aq, ak, av, bias_tab, norm_w, batch, seq):
    nb = seq // ATTN_QB
    qblk = pl.BlockSpec((ATTN_QB, ATTN_WIDTH), lambda b, i: (b * nb + i, 0))

    def kblk(back):
        return pl.BlockSpec((ATTN_QB, ATTN_WIDTH),
                            lambda b, i: (b * nb + jnp.maximum(i - back, 0), 0))

    return pl.pallas_call(
        _attn_kernel,
        grid=(batch, nb),
        in_specs=[qblk, kblk(2), kblk(1), kblk(0), kblk(2), kblk(1), kblk(0),
                  pl.BlockSpec((ATTN_HEADS, ATTN_QB, ATTN_KB), lambda b, i: (0, 0, 0)),
                  pl.BlockSpec((1, ATTN_WIDTH), lambda b, i: (0, 0))],
        out_specs=qblk,
        out_shape=jax.ShapeDtypeStruct(aq.shape, BF16),
        compiler_params=_cparams(("parallel", "arbitrary")),
        name="band_attn",
    )(aq, ak, ak, ak, av, av, av, bias_tab, norm_w)


BIAS_ROW = ATTN_QB + ATTN_KB


def _bias_kernel(base_ref, o_ref):
    row = lax.broadcasted_iota(jnp.int32, (ATTN_QB, BIAS_ROW), 0)
    x = jnp.broadcast_to(base_ref[0], (ATTN_QB, BIAS_ROW))
    x = pltpu.roll(x, BIAS_ROW - ATTN_QB, 1)
    for bit in range(int(math.log2(ATTN_QB))):
        x = jnp.where(((row >> bit) & 1) == 1, pltpu.roll(x, 1 << bit, 1), x)
    r = lax.broadcasted_iota(jnp.int32, (ATTN_QB, ATTN_KB), 0)
    c = lax.broadcasted_iota(jnp.int32, (ATTN_QB, ATTN_KB), 1)
    dchunk = (2 * ATTN_QB) // CHUNK + r // CHUNK - c // CHUNK
    valid = (dchunk >= 0) & (dchunk <= LEFT_CHUNKS)
    o_ref[0] = jnp.where(valid, x[:, :ATTN_KB] * LOG2_E, NEG_BIG)


def _bias_table(rel_bias):
    H = rel_bias.shape[0]
    n_hi = 2 * ATTN_QB - MAX_REL + ATTN_QB
    n_lo = BIAS_ROW - n_hi - (2 * MAX_REL + 1)
    rb = rel_bias.astype(F32)
    base = jnp.concatenate([jnp.broadcast_to(rb[:, -1:], (H, n_hi)), rb[:, ::-1],
                            jnp.broadcast_to(rb[:, :1], (H, n_lo))], axis=1)
    return pl.pallas_call(
        _bias_kernel,
        grid=(H,),
        in_specs=[pl.BlockSpec((1, 1, BIAS_ROW), lambda h: (h, 0, 0))],
        out_specs=pl.BlockSpec((1, ATTN_QB, ATTN_KB), lambda h: (h, 0, 0)),
        out_shape=jax.ShapeDtypeStruct((H, ATTN_QB, ATTN_KB), F32),
        compiler_params=_cparams(("parallel",)),
        name="rel_bias_table",
    )(base.reshape(H, 1, BIAS_ROW))


def _outproj_router_kernel(yr_ref, ya_ref, x_ref, wo_ref, nw_ref, wr_ref,
                           x1_ref, h2_ref, route_ref, cnt_ref, run_ref, wob_ref):
    @pl.when(pl.program_id(0) == 0)
    def _():
        run_ref[...] = jnp.zeros_like(run_ref)
        wob_ref[...] = wo_ref[...].astype(BF16)

    acc = jnp.dot(yr_ref[...], wob_ref[0:HGRN_WIDTH, :], preferred_element_type=F32)
    acc = acc + jnp.dot(ya_ref[...], wob_ref[HGRN_WIDTH:, :], preferred_element_type=F32)
    x1 = x_ref[...] + acc
    x1_ref[...] = x1
    h = x1 * lax.rsqrt(jnp.mean(x1 * x1, axis=-1, keepdims=True) + EPS) * nw_ref[...]
    h_hi = h.astype(BF16)
    h2_ref[:, :D_MODEL] = h_hi.astype(F32)

    h_lo = (h - h_hi.astype(F32)).astype(BF16)
    hi_prod = jnp.dot(h_hi, wr_ref[...], preferred_element_type=F32)
    lg = (hi_prod[:, :LANES] + hi_prod[:, LANES:]
          + jnp.dot(h_lo, wr_ref[:, :LANES], preferred_element_type=F32))
    lane = lax.broadcasted_iota(jnp.int32, lg.shape, 1)
    lanef = lane.astype(F32)
    is_g = (lane >= N_EXPERTS) & (lane < N_EXPERTS + N_GROUPS)
    g_max = jnp.max(jnp.where(is_g, lg, -jnp.inf), axis=-1, keepdims=True)
    g_sum = jnp.sum(jnp.where(is_g, jnp.exp(lg - g_max), 0.0), axis=-1, keepdims=True)
    g_gate = 1.0 / g_sum
    g_idx = jnp.min(jnp.where(is_g & (lg == g_max), lanef - N_EXPERTS, 1e9),
                    axis=-1, keepdims=True)
    in_grp = (lane < N_EXPERTS) & ((lane // EXPERTS_PER_GROUP).astype(F32) == g_idx)
    v1 = jnp.max(jnp.where(in_grp, lg, -jnp.inf), axis=-1, keepdims=True)
    i1 = jnp.min(jnp.where(in_grp & (lg == v1), lanef, 1e9), axis=-1, keepdims=True)
    rest = in_grp & (lanef != i1)
    v2 = jnp.max(jnp.where(rest, lg, -jnp.inf), axis=-1, keepdims=True)
    i2 = jnp.min(jnp.where(rest & (lg == v2), lanef, 1e9), axis=-1, keepdims=True)
    e2 = jnp.exp(v2 - v1)
    gate1 = g_gate / (1.0 + e2)
    gate2 = gate1 * e2

    hit = lanef == g_idx
    onehot = jnp.where(hit, 1.0, 0.0)
    n = lg.shape[0]
    rr = lax.broadcasted_iota(jnp.int32, (n, n), 0)
    cc = lax.broadcasted_iota(jnp.int32, (n, n), 1)
    earlier = jnp.where(cc < rr, 1.0, 0.0).astype(BF16)
    before = jnp.dot(earlier, onehot.astype(BF16), preferred_element_type=F32) + run_ref[...]
    rank = jnp.sum(jnp.where(hit, before, 0.0), axis=-1, keepdims=True)
    run_ref[...] += jnp.sum(onehot, axis=0, keepdims=True)
    cnt_ref[...] = run_ref[...]

    rec = jnp.zeros_like(lg)
    for slot, val in ((R_E1, i1), (R_E2, i2), (R_G1, gate1), (R_G2, gate2),
                      (R_GROUP, g_idx), (R_RANK, rank)):
        rec = jnp.where(lane == slot, val, rec)
    route_ref[...] = rec
    h2_ref[:, D_MODEL:] = rec


def _outproj_router(y_rec, y_att, x2d, w_out, layer, nw, w_router):
    T = x2d.shape[0]
    row = lambda i: (i, 0)
    const = lambda i: (0, 0)
    return pl.pallas_call(
        _outproj_router_kernel,
        grid=(T // TOK_TILE,),
        in_specs=[pl.BlockSpec((TOK_TILE, HGRN_WIDTH), row),
                  pl.BlockSpec((TOK_TILE, ATTN_WIDTH), row),
                  pl.BlockSpec((TOK_TILE, D_MODEL), row),
                  pl.BlockSpec((None, D_MODEL, D_MODEL), lambda i: (layer, 0, 0)),
                  pl.BlockSpec((1, D_MODEL), const),
                  pl.BlockSpec((D_MODEL, 2 * LANES), const)],
        out_specs=[pl.BlockSpec((TOK_TILE, D_MODEL), row),
                   pl.BlockSpec((TOK_TILE, ROW_W), row),
                   pl.BlockSpec((TOK_TILE, LANES), row),
                   pl.BlockSpec((1, LANES), const)],
        out_shape=[jax.ShapeDtypeStruct((T, D_MODEL), F32),
                   jax.ShapeDtypeStruct((T, ROW_W), F32),
                   jax.ShapeDtypeStruct((T, LANES), F32),
                   jax.ShapeDtypeStruct((1, LANES), F32)],
        scratch_shapes=[pltpu.VMEM((1, LANES), F32), pltpu.VMEM((D_MODEL, D_MODEL), BF16)],
        compiler_params=_cparams(("arbitrary",)),
        name="outproj_router",
    )(y_rec, y_att, x2d, w_out, nw, w_router)


def _row_copy(src_ref, src_row, dst_ref, dst_row, sem):
    return pltpu.make_async_copy(src_ref.at[pl.ds(src_row, 1), :],
                                 dst_ref.at[pl.ds(dst_row, 1), :], sem)


def _dispatch_kernel(pos_ref, h_ref, hs_init_ref, hs_ref, sem):
    del hs_init_ref

    def issue(t, carry):
        _row_copy(h_ref, t, hs_ref, pos_ref[t], sem).start()
        return carry

    lax.fori_loop(0, DISP_TILE, issue, 0, unroll=8)

    pltpu.make_async_copy(h_ref, hs_ref.at[pl.ds(0, DISP_TILE), :], sem).wait()


def _dispatch(pos, h2, n_rows):
    T = h2.shape[0]
    return pl.pallas_call(
        _dispatch_kernel,
        grid=(T // DISP_TILE,),
        in_specs=[pl.BlockSpec((DISP_TILE,), lambda i: (i,), memory_space=pltpu.SMEM),
                  pl.BlockSpec((DISP_TILE, ROW_W), lambda i: (i, 0)),
                  pl.BlockSpec(memory_space=pl.ANY)],
        out_specs=pl.BlockSpec(memory_space=pl.ANY),
        out_shape=jax.ShapeDtypeStruct((n_rows, ROW_W), F32),
        scratch_shapes=[pltpu.SemaphoreType.DMA(())],
        input_output_aliases={2: 0},
        compiler_params=_cparams(("arbitrary",)),
        name="moe_dispatch",
    )(pos, h2, jnp.zeros((n_rows, ROW_W), F32))


def _expert_kernel(tg_ref, nv_ref, hs_ref, w1_ref, w3_ref, w2_ref, ys_ref, hb_ref, cw_ref):
    j = pl.program_id(0)
    e = pl.program_id(1)
    valid = j < nv_ref[0]

    @pl.when(valid & (e == 0))
    def _():
        hb_ref[...] = hs_ref[:, :D_MODEL].astype(BF16)
        rec = hs_ref[:, D_MODEL:]
        lane = lax.broadcasted_iota(jnp.int32, rec.shape, 1)

        def field(slot):
            return jnp.sum(jnp.where(lane == slot, rec, 0.0), axis=-1, keepdims=True)

        first = (tg_ref[j] * EXPERTS_PER_GROUP).astype(F32)
        lanef = lane.astype(F32)
        cw_ref[...] = (jnp.where(field(R_E1) - first == lanef, field(R_G1), 0.0)
                       + jnp.where(field(R_E2) - first == lanef, field(R_G2), 0.0))

    @pl.when(valid)
    def _():
        h = hb_ref[...]
        cw = cw_ref[...]
        lane = lax.broadcasted_iota(jnp.int32, cw.shape, 1)
        hid = []
        for k in range(MOE_EPS):
            gate = jnp.sum(jnp.where(lane == e * MOE_EPS + k, cw, 0.0), axis=-1, keepdims=True)
            a = jnp.dot(h, w1_ref[k].astype(BF16), preferred_element_type=F32)
            b = jnp.dot(h, w3_ref[k].astype(BF16), preferred_element_type=F32)
            hid.append((_silu(a) * b * gate).astype(BF16))
        w2 = w2_ref[...].astype(BF16).reshape(MOE_EPS * EXPERT_FF, D_MODEL)
        y = jnp.dot(jnp.concatenate(hid, axis=-1), w2, preferred_element_type=F32)

        @pl.when(e == 0)
        def _():
            ys_ref[...] = y

        @pl.when(e > 0)
        def _():
            ys_ref[...] += y

    @pl.when(jnp.logical_not(valid) & (e == 0))
    def _():
        ys_ref[...] = jnp.zeros_like(ys_ref)


def _experts(tile_group, n_valid, hs, w1, w3, w2, layer):
    n_rows = hs.shape[0]
    rows_in = lambda j, e, tg, nv: (jnp.minimum(j, nv[0] - 1), 0)
    steps = EXPERTS_PER_GROUP // MOE_EPS

    def wsel(j, e, tg, nv):
        last = nv[0] - 1
        blk = jnp.where(j <= last, tg[j] * steps + e, tg[last] * steps + steps - 1)
        return (layer, blk, 0, 0)

    return pl.pallas_call(
        _expert_kernel,
        grid_spec=pltpu.PrefetchScalarGridSpec(
            num_scalar_prefetch=2,
            grid=(n_rows // MOE_ROWS, steps),
            in_specs=[pl.BlockSpec((MOE_ROWS, ROW_W), rows_in),
                      pl.BlockSpec((None, MOE_EPS, D_MODEL, EXPERT_FF), wsel),
                      pl.BlockSpec((None, MOE_EPS, D_MODEL, EXPERT_FF), wsel),
                      pl.BlockSpec((None, MOE_EPS, EXPERT_FF, D_MODEL), wsel)],
            out_specs=pl.BlockSpec((MOE_ROWS, D_MODEL), lambda j, e, tg, nv: (j, 0)),
            scratch_shapes=[pltpu.VMEM((MOE_ROWS, D_MODEL), BF16),
                            pltpu.VMEM((MOE_ROWS, LANES), F32)]),
        out_shape=jax.ShapeDtypeStruct((n_rows, D_MODEL), F32),
        compiler_params=_cparams(("arbitrary", "arbitrary")),
        name="moe_experts",
    )(tile_group, n_valid, hs, w1, w3, w2)


def _combine_kernel(pos_ref, x1_ref, fw_ref, ys_ref, o_ref, buf_ref, sem, *, final_norm):
    i = pl.program_id(0)
    n = pl.num_programs(0)

    def gather(tile, slot, start):
        if not start:
            pltpu.make_async_copy(ys_ref.at[pl.ds(0, COMB_TILE), :], buf_ref.at[slot],
                                  sem.at[slot]).wait()
            return
        base = tile * COMB_TILE

        def body(t, carry):
            _row_copy(ys_ref, pos_ref[base + t], buf_ref.at[slot], t, sem.at[slot]).start()
            return carry

        lax.fori_loop(0, COMB_TILE, body, 0, unroll=8)

    @pl.when(i == 0)
    def _():
        gather(0, 0, True)

    for slot in range(2):
        @pl.when((i + 1 < n) & ((i + 1) % 2 == slot))
        def _():
            gather(i + 1, slot, True)

    for slot in range(2):
        @pl.when(i % 2 == slot)
        def _():
            gather(i, slot, False)
            x2 = x1_ref[...] + buf_ref[slot]
            if final_norm:
                x2 = x2 * lax.rsqrt(jnp.mean(x2 * x2, axis=-1, keepdims=True) + EPS) * fw_ref[...]
            o_ref[...] = x2


def _combine(pos, x1, final_w, ys, final_norm):
    T = x1.shape[0]
    row = lambda i: (i, 0)
    return pl.pallas_call(
        functools.partial(_combine_kernel, final_norm=final_norm),
        grid=(T // COMB_TILE,),
        in_specs=[pl.BlockSpec(memory_space=pltpu.SMEM),
                  pl.BlockSpec((COMB_TILE, D_MODEL), row),
                  pl.BlockSpec((1, D_MODEL), lambda i: (0, 0)),
                  pl.BlockSpec(memory_space=pl.ANY)],
        out_specs=pl.BlockSpec((COMB_TILE, D_MODEL), row),
        out_shape=jax.ShapeDtypeStruct((T, D_MODEL), F32),
        scratch_shapes=[pltpu.VMEM((2, COMB_TILE, D_MODEL), F32),
                        pltpu.SemaphoreType.DMA((2,))],
        compiler_params=_cparams(("arbitrary",)),
        name="moe_combine",
    )(pos, x1, final_w, ys)


def _sorted_layout(route, counts, n_tiles):
    cnt = counts[0, :N_GROUPS].astype(jnp.int32)
    padded = (cnt + MOE_ROWS - 1) // MOE_ROWS * MOE_ROWS
    ends = jnp.cumsum(padded)
    offs = ends - padded
    tile_start = jnp.arange(n_tiles, dtype=jnp.int32) * MOE_ROWS
    tile_group = jnp.minimum(
        jnp.sum(ends[None, :] <= tile_start[:, None], axis=1), N_GROUPS - 1).astype(jnp.int32)
    n_valid = (ends[-1:] // MOE_ROWS).astype(jnp.int32)
    group = route[:, R_GROUP].astype(jnp.int32)
    ids = jnp.arange(N_GROUPS, dtype=jnp.int32)
    seg = jnp.sum(jnp.where(group[:, None] == ids[None, :], offs[None, :], 0), axis=1)
    return seg + route[:, R_RANK].astype(jnp.int32), tile_group, n_valid


def kernel(x, norm_mix_w, w_in, hgrn_lower_bounds, hgrn_gnorm_w, attn_rel_bias, attn_norm_w,
           w_out, norm_ffn_w, moe_w_group, moe_w_expert, moe_w1, moe_w3, moe_w2, final_norm_w):
    B, S, D = x.shape
    T = B * S
    lb_all = jnp.cumsum(jax.nn.softmax(hgrn_lower_bounds.astype(F32), axis=0), axis=0)
    lb_all = lb_all - lb_all[0]
    log_lb = jnp.log(lb_all)
    log_1m_lb = jnp.log1p(-lb_all)

    xt = x.reshape(T, D)
    for l in range(DEPTH):
        q, lf, bc, v, g, aq, ak, av = _norm_inproj(
            xt, norm_mix_w[l][None], w_in, l, log_lb[l][None], log_1m_lb[l][None])
        y_rec = _hgrn(q, lf, bc, v, g, hgrn_gnorm_w[l][None], B, S)
        y_att = _attention(aq, ak, av, _bias_table(attn_rel_bias[l]), attn_norm_w[l][None], B, S)
        w_r = jnp.concatenate(
            [moe_w_expert[l], moe_w_group[l],
             jnp.zeros((D, LANES - N_EXPERTS - N_GROUPS), F32)], axis=1)
        w_r_hi = w_r.astype(BF16)
        w_router = jnp.concatenate([w_r_hi, (w_r - w_r_hi.astype(F32)).astype(BF16)], axis=1)
        x1, h2, route, counts = _outproj_router(y_rec, y_att, xt, w_out, l,
                                                norm_ffn_w[l][None], w_router)
        n_tiles = T // MOE_ROWS + N_GROUPS
        pos, tile_group, n_valid = _sorted_layout(route, counts, n_tiles)
        hs = _dispatch(pos, h2, n_tiles * MOE_ROWS)
        ys = _experts(tile_group, n_valid, hs, moe_w1, moe_w3, moe_w2, l)
        xt = _combine(pos, x1, final_norm_w[None], ys, final_norm=(l == DEPTH - 1))
    return xt.reshape(B, S, D)
```

```python
import functools
import math

import jax
import jax.numpy as jnp
import numpy as np
from jax import lax
from jax.experimental import pallas as pl
from jax.experimental.pallas import tpu as pltpu

F32 = jnp.float32
BF16 = jnp.bfloat16

D_MODEL = 1024
DEPTH = 2
CHUNK = 64
HGRN_WIDTH = 512
HGRN_HEADS = 4
HGRN_D = 128
ATTN_WIDTH = 512
ATTN_HEADS = 8
ATTN_HEAD_DIM = 64
LEFT_CHUNKS = 8
MAX_REL = 128
N_GROUPS = 4
EXPERTS_PER_GROUP = 8
N_EXPERTS = 32
EXPERT_FF = 256
IN_COLS = 4 * HGRN_WIDTH + 3 * ATTN_WIDTH
EPS = 1e-6

LANES = 128
SUBLANES = 8
VMEM_LIMIT = 56 * 1024 * 1024
NEG_BIG = -1e30
LOG2_E = 1.4426950408889634

TOK_TILE = 512
ATTN_QB = 256
ATTN_KB = 3 * ATTN_QB
HGRN_TB = 512
MOE_ROWS = 1024
MOE_EPS = 4
ROW_TILE = D_MODEL // LANES
H_BLOCKS = D_MODEL // 2 // LANES
REC_BLOCK = H_BLOCKS
DISP_TILE = 1024
COMB_TILE = 512
R_E1, R_E2, R_G1, R_G2, R_GROUP, R_RANK = range(6)


def _cparams(sem):
    return pltpu.CompilerParams(dimension_semantics=sem, vmem_limit_bytes=VMEM_LIMIT)


def _silu(x):
    return x * jax.nn.sigmoid(x)


def _chunk_cumsum(x, row_in_chunk):
    acc = x
    step = 1
    while step < CHUNK:
        acc = acc + jnp.where(row_in_chunk >= step, pltpu.roll(acc, step, 0), 0.0)
        step *= 2
    return acc


def _norm_inproj_kernel(x_ref, nw_ref, w_ref, la_ref, lc_ref,
                        q_ref, lf_ref, bc_ref, v_ref, g_ref, aq_ref, ak_ref, av_ref, wbf_ref):
    @pl.when(pl.program_id(0) == 0)
    def _():
        wbf_ref[...] = w_ref[...].astype(BF16)

    x = x_ref[...]
    ms = jnp.mean(x * x, axis=-1, keepdims=True)
    h = (x * lax.rsqrt(ms + EPS) * nw_ref[...]).astype(BF16)

    def proj(j):
        return jnp.dot(h, wbf_ref[:, j * 512:(j + 1) * 512], preferred_element_type=F32)

    q_ref[...] = _silu(proj(0)).astype(BF16)
    z = proj(1)
    log_sig = jnp.minimum(z, 0.0) - jnp.log(1.0 + jnp.exp(-jnp.abs(z)))
    a = la_ref[...]
    b = lc_ref[...] + log_sig
    lf2 = (jnp.maximum(a, b) + jnp.log(1.0 + jnp.exp(-jnp.abs(a - b)))) * LOG2_E
    lf_ref[...] = lf2
    row_in_chunk = lax.broadcasted_iota(jnp.int32, lf2.shape, 0) & (CHUNK - 1)
    bc_ref[...] = _chunk_cumsum(lf2, row_in_chunk)
    v_ref[...] = proj(2).astype(BF16)
    g_ref[...] = _silu(proj(3)).astype(BF16)
    aq_ref[...] = (proj(4) * (LOG2_E / math.sqrt(ATTN_HEAD_DIM))).astype(BF16)
    ak_ref[...] = proj(5).astype(BF16)
    av_ref[...] = proj(6).astype(BF16)


def _norm_inproj(x2d, nw, w_in, layer, log_lb, log_1m_lb):
    T = x2d.shape[0]
    row = lambda i: (i, 0)
    const = lambda i: (0, 0)
    out_bf = jax.ShapeDtypeStruct((T, 512), BF16)
    out_f32 = jax.ShapeDtypeStruct((T, 512), F32)
    blk = pl.BlockSpec((TOK_TILE, 512), row)
    return pl.pallas_call(
        _norm_inproj_kernel,
        grid=(T // TOK_TILE,),
        in_specs=[pl.BlockSpec((TOK_TILE, D_MODEL), row),
                  pl.BlockSpec((1, D_MODEL), const),
                  pl.BlockSpec((None, D_MODEL, IN_COLS), lambda i: (layer, 0, 0),
                               pipeline_mode=pl.Buffered(1)),
                  pl.BlockSpec((1, 512), const),
                  pl.BlockSpec((1, 512), const)],
        out_specs=[blk] * 8,
        out_shape=[out_bf, out_f32, out_f32, out_bf, out_bf, out_bf, out_bf, out_bf],
        scratch_shapes=[pltpu.VMEM((D_MODEL, IN_COLS), BF16)],
        compiler_params=_cparams(("arbitrary",)),
        name="norm_inproj",
    )(x2d, nw, w_in, log_lb, log_1m_lb)


def _level_ref(bcum, half):
    if half >= 8:
        b3 = bcum.reshape(CHUNK // (2 * half), 2 * half, HGRN_D)
        ref = jnp.broadcast_to(b3[:, half - 1:half, :], b3.shape)
        return ref.reshape(CHUNK, HGRN_D)
    b3 = bcum.reshape(CHUNK // 8, 8, HGRN_D)
    if half == 4:
        ref = jnp.broadcast_to(b3[:, 3:4, :], b3.shape)
    else:
        sub = lax.broadcasted_iota(jnp.int32, b3.shape, 1)
        ref = jnp.where(sub < 4,
                        jnp.broadcast_to(b3[:, 1:2, :], b3.shape),
                        jnp.broadcast_to(b3[:, 5:6, :], b3.shape))
    return ref.reshape(CHUNK, HGRN_D)


def _dot_nt(a, b):
    return lax.dot_general(a, b, (((1,), (1,)), ((), ())), preferred_element_type=F32)


def _dot_tn(a, b):
    return lax.dot_general(a, b, (((0,), (0,)), ((), ())), preferred_element_type=F32)


HGRN_LEVELS = (32, 16, 8, 4, 2)


def _hgrn_kernel(q_ref, lf_ref, bc_ref, v_ref, g_ref, gw_ref, o_ref, *st_refs):
    @pl.when(pl.program_id(1) == 0)
    def _():
        for st_ref in st_refs:
            st_ref[...] = jnp.zeros_like(st_ref)

    row = lax.broadcasted_iota(jnp.int32, (CHUNK, HGRN_D), 0)
    sign = {half: jnp.where((row & half) != 0, 1.0, -1.0) for half in HGRN_LEVELS}
    r64 = lax.broadcasted_iota(jnp.int32, (CHUNK, CHUNK), 0)
    c64 = lax.broadcasted_iota(jnp.int32, (CHUNK, CHUNK), 1)
    differ = jnp.where(r64 > c64, r64 ^ c64, 0)
    level = sum((differ >= half).astype(jnp.int32) for half in (1,) + HGRN_LEVELS)

    def chunk_body(c, carry):
        rows = pl.ds(pl.multiple_of(c * CHUNK, CHUNK), CHUNK)
        heads = range(HGRN_HEADS)
        cols = [slice(h * HGRN_D, (h + 1) * HGRN_D) for h in heads]
        q = [q_ref[rows, cs].astype(F32) for cs in cols]
        f = [jnp.exp2(lf_ref[rows, cs]) for cs in cols]
        k = [1.0 - x for x in f]
        bcum = [bc_ref[rows, cs] for cs in cols]

        scores = [_dot_nt((q[h] * f[h]).astype(BF16), k[h].astype(BF16)) for h in heads]
        scores = [jnp.where(level == 1, s, 0.0) for s in scores]
        for half in HGRN_LEVELS:
            lvl = int(math.log2(half)) + 1
            for h in heads:
                fac = jnp.exp2((bcum[h] - _level_ref(bcum[h], half)) * sign[half])
                s_l = _dot_nt((q[h] * fac).astype(BF16), (k[h] * fac).astype(BF16))
                scores[h] = jnp.where(level == lvl, s_l, scores[h])

        v = [v_ref[rows, cs] for cs in cols]
        st = [st_ref[...] for st_ref in st_refs]
        inter = [_dot_nt((q[h] * jnp.exp2(bcum[h])).astype(BF16), st[h].astype(BF16))
                 for h in heads]
        intra = [jnp.dot(scores[h].astype(BF16), v[h], preferred_element_type=F32)
                 for h in heads]
        for h in heads:
            b_last = bcum[h][CHUNK - 1:CHUNK, :]
            k_dec = (k[h] * jnp.exp2(b_last - bcum[h])).astype(BF16)
            st_refs[h][...] = st[h] * jnp.exp2(b_last) + _dot_tn(v[h], k_dec)
        for h in heads:
            diag = jnp.sum(q[h] * k[h], axis=-1, keepdims=True) * v[h].astype(F32)
            o = intra[h] + diag + inter[h]
            o = o * lax.rsqrt(jnp.mean(o * o, axis=-1, keepdims=True) + EPS)
            o = o * gw_ref[:, cols[h]] * g_ref[rows, cols[h]].astype(F32)
            o_ref[rows, cols[h]] = o.astype(BF16)
        return carry

    lax.fori_loop(0, HGRN_TB // CHUNK, chunk_body, 0, unroll=2)


def _hgrn(q, lf, bc, v, g, gnorm_w, batch, seq):
    blk = pl.BlockSpec((HGRN_TB, HGRN_WIDTH), lambda b, i: (b * (seq // HGRN_TB) + i, 0))
    return pl.pallas_call(
        _hgrn_kernel,
        grid=(batch, seq // HGRN_TB),
        in_specs=[blk] * 5 + [pl.BlockSpec((1, HGRN_WIDTH), lambda b, i: (0, 0))],
        out_specs=blk,
        out_shape=jax.ShapeDtypeStruct(q.shape, BF16),
        scratch_shapes=[pltpu.VMEM((HGRN_D, HGRN_D), F32)] * HGRN_HEADS,
        compiler_params=_cparams(("parallel", "arbitrary")),
        name="hgrn2",
    )(q, lf, bc, v, g, gnorm_w)


def _attn_kernel(q_ref, k0_ref, k1_ref, k2_ref, v0_ref, v1_ref, v2_ref, bias_ref, nw_ref, o_ref):
    i = pl.program_id(1)
    lane = lax.broadcasted_iota(jnp.int32, (1, LANES), 1)
    lo = lane < ATTN_HEAD_DIM

    def block(pad_mask):
        def scores(head):
            cols = slice(head // 2 * LANES, (head // 2 + 1) * LANES)
            qp = q_ref[:, cols]
            kp = jnp.concatenate([k0_ref[:, cols], k1_ref[:, cols], k2_ref[:, cols]], axis=0)
            qh = jnp.where(lo if head % 2 == 0 else ~lo, qp, jnp.zeros_like(qp))
            return _dot_nt(qh, kp)

        s_next = scores(0)
        outs = []
        for head in range(ATTN_HEADS):
            s = s_next + bias_ref[head]
            if head + 1 < ATTN_HEADS:
                s_next = scores(head + 1)
            if pad_mask is not None:
                s = s + pad_mask
            e = jnp.exp2(s - jnp.max(s, axis=-1, keepdims=True))
            cols = slice(head // 2 * LANES, (head // 2 + 1) * LANES)
            vp = jnp.concatenate([v0_ref[:, cols], v1_ref[:, cols], v2_ref[:, cols]], axis=0)
            pv = jnp.dot(e.astype(BF16),
                         jnp.where(lo if head % 2 == 0 else ~lo, vp, jnp.ones_like(vp)),
                         preferred_element_type=F32)
            outs.append(pv / pltpu.roll(pv, ATTN_HEAD_DIM, 1))
            if head % 2 == 1:
                o = jnp.where(lo, outs[0], outs[1])
                outs = []
                o2 = o * o
                ms0 = jnp.sum(jnp.where(lo, o2, 0.0), axis=-1, keepdims=True)
                ms1 = jnp.sum(jnp.where(lo, 0.0, o2), axis=-1, keepdims=True)
                ms = jnp.where(lo, ms0, ms1) * (1.0 / ATTN_HEAD_DIM)
                o_ref[:, cols] = (o * lax.rsqrt(ms + EPS) * nw_ref[:, cols]).astype(BF16)

    @pl.when(i >= 2)
    def _():
        block(None)

    @pl.when(i < 2)
    def _():
        col = lax.broadcasted_iota(jnp.int32, (1, ATTN_KB), 1)
        block(jnp.where(col >= (2 - i) * ATTN_QB, 0.0, NEG_BIG))


def _attention(aq, ak, av, bias_tab, norm_w, batch, seq):
    nb = seq // ATTN_QB
    qblk = pl.BlockSpec((ATTN_QB, ATTN_WIDTH), lambda b, i: (b * nb + i, 0))

    def kblk(back):
        return pl.BlockSpec((ATTN_QB, ATTN_WIDTH),
                            lambda b, i: (b * nb + jnp.maximum(i - back, 0), 0))

    return pl.pallas_call(
        _attn_kernel,
        grid=(batch, nb),
        in_specs=[qblk, kblk(2), kblk(1), kblk(0), kblk(2), kblk(1), kblk(0),
                  pl.BlockSpec((ATTN_HEADS, ATTN_QB, ATTN_KB), lambda b, i: (0, 0, 0)),
                  pl.BlockSpec((1, ATTN_WIDTH), lambda b, i: (0, 0))],
        out_specs=qblk,
        out_shape=jax.ShapeDtypeStruct(aq.shape, BF16),
        compiler_params=_cparams(("parallel", "arbitrary")),
        name="band_attn",
    )(aq, ak, ak, ak, av, av, av, bias_tab, norm_w)


BIAS_ROW = ATTN_QB + ATTN_KB


def _bias_kernel(base_ref, o_ref):
    row = lax.broadcasted_iota(jnp.int32, (ATTN_QB, BIAS_ROW), 0)
    x = jnp.broadcast_to(base_ref[0], (ATTN_QB, BIAS_ROW))
    x = pltpu.roll(x, BIAS_ROW - ATTN_QB, 1)
    for bit in range(int(math.log2(ATTN_QB))):
        x = jnp.where(((row >> bit) & 1) == 1, pltpu.roll(x, 1 << bit, 1), x)
    r = lax.broadcasted_iota(jnp.int32, (ATTN_QB, ATTN_KB), 0)
    c = lax.broadcasted_iota(jnp.int32, (ATTN_QB, ATTN_KB), 1)
    dchunk = (2 * ATTN_QB) // CHUNK + r // CHUNK - c // CHUNK
    valid = (dchunk >= 0) & (dchunk <= LEFT_CHUNKS)
    o_ref[0] = jnp.where(valid, x[:, :ATTN_KB] * LOG2_E, NEG_BIG)


def _bias_table(rel_bias):
    H = rel_bias.shape[0]
    n_hi = 2 * ATTN_QB - MAX_REL + ATTN_QB
    n_lo = BIAS_ROW - n_hi - (2 * MAX_REL + 1)
    rb = rel_bias.astype(F32)
    base = jnp.concatenate([jnp.broadcast_to(rb[:, -1:], (H, n_hi)), rb[:, ::-1],
                            jnp.broadcast_to(rb[:, :1], (H, n_lo))], axis=1)
    return pl.pallas_call(
        _bias_kernel,
        grid=(H,),
        in_specs=[pl.BlockSpec((1, 1, BIAS_ROW), lambda h: (h, 0, 0))],
        out_specs=pl.BlockSpec((1, ATTN_QB, ATTN_KB), lambda h: (h, 0, 0)),
        out_shape=jax.ShapeDtypeStruct((H, ATTN_QB, ATTN_KB), F32),
        compiler_params=_cparams(("parallel",)),
        name="rel_bias_table",
    )(base.reshape(H, 1, BIAS_ROW))


def _outproj_router_kernel(yr_ref, ya_ref, x_ref, wo_ref, nw_ref, wr_ref,
                           x1_ref, h2_ref, route_ref, cnt_ref, run_ref, wob_ref):
    @pl.when(pl.program_id(0) == 0)
    def _():
        run_ref[...] = jnp.zeros_like(run_ref)
        wob_ref[...] = wo_ref[...].astype(BF16)

    acc = jnp.dot(yr_ref[...], wob_ref[0:HGRN_WIDTH, :], preferred_element_type=F32)
    acc = acc + jnp.dot(ya_ref[...], wob_ref[HGRN_WIDTH:, :], preferred_element_type=F32)
    x1 = x_ref[...] + acc
    x1_ref[...] = x1
    h = x1 * lax.rsqrt(jnp.mean(x1 * x1, axis=-1, keepdims=True) + EPS) * nw_ref[...]
    h_hi = h.astype(BF16)
    half = D_MODEL // 2
    pairs = pltpu.pack_elementwise([h[:, :half], h[:, half:]], packed_dtype=BF16)
    n_tok = pairs.shape[0]
    for c in range(H_BLOCKS):
        h2_ref[_lane_block_rows(c, n_tok), :] = pairs[:, c * LANES:(c + 1) * LANES]
    for c in range(REC_BLOCK + 1, ROW_TILE):
        h2_ref[_lane_block_rows(c, n_tok), :] = jnp.zeros((n_tok, LANES), jnp.uint32)

    h_lo = (h - h_hi.astype(F32)).astype(BF16)
    hi_prod = jnp.dot(h_hi, wr_ref[...], preferred_element_type=F32)
    lg = (hi_prod[:, :LANES] + hi_prod[:, LANES:]
          + jnp.dot(h_lo, wr_ref[:, :LANES], preferred_element_type=F32))
    lane = lax.broadcasted_iota(jnp.int32, lg.shape, 1)
    lanef = lane.astype(F32)
    is_g = (lane >= N_EXPERTS) & (lane < N_EXPERTS + N_GROUPS)
    g_max = jnp.max(jnp.where(is_g, lg, -jnp.inf), axis=-1, keepdims=True)
    g_sum = jnp.sum(jnp.where(is_g, jnp.exp(lg - g_max), 0.0), axis=-1, keepdims=True)
    g_gate = 1.0 / g_sum
    g_idx = jnp.min(jnp.where(is_g & (lg == g_max), lanef - N_EXPERTS, 1e9),
                    axis=-1, keepdims=True)
    in_grp = (lane < N_EXPERTS) & ((lane // EXPERTS_PER_GROUP).astype(F32) == g_idx)
    v1 = jnp.max(jnp.where(in_grp, lg, -jnp.inf), axis=-1, keepdims=True)
    i1 = jnp.min(jnp.where(in_grp & (lg == v1), lanef, 1e9), axis=-1, keepdims=True)
    rest = in_grp & (lanef != i1)
    v2 = jnp.max(jnp.where(rest, lg, -jnp.inf), axis=-1, keepdims=True)
    i2 = jnp.min(jnp.where(rest & (lg == v2), lanef, 1e9), axis=-1, keepdims=True)
    e2 = jnp.exp(v2 - v1)
    gate1 = g_gate / (1.0 + e2)
    gate2 = gate1 * e2

    hit = lanef == g_idx
    onehot = jnp.where(hit, 1.0, 0.0)
    n = lg.shape[0]
    rr = lax.broadcasted_iota(jnp.int32, (n, n), 0)
    cc = lax.broadcasted_iota(jnp.int32, (n, n), 1)
    earlier = jnp.where(cc < rr, 1.0, 0.0).astype(BF16)
    before = jnp.dot(earlier, onehot.astype(BF16), preferred_element_type=F32) + run_ref[...]
    rank = jnp.sum(jnp.where(hit, before, 0.0), axis=-1, keepdims=True)
    run_ref[...] += jnp.sum(onehot, axis=0, keepdims=True)
    cnt_ref[...] = run_ref[...]

    rec = jnp.zeros_like(lg)
    for slot, val in ((R_E1, i1), (R_E2, i2), (R_G1, gate1), (R_G2, gate2),
                      (R_GROUP, g_idx), (R_RANK, rank)):
        rec = jnp.where(lane == slot, val, rec)
    route_ref[...] = rec
    h2_ref[_lane_block_rows(REC_BLOCK, n_tok), :] = pltpu.bitcast(rec, jnp.uint32)


def _outproj_router(y_rec, y_att, x2d, w_out, layer, nw, w_router):
    T = x2d.shape[0]
    row = lambda i: (i, 0)
    const = lambda i: (0, 0)
    return pl.pallas_call(
        _outproj_router_kernel,
        grid=(T // TOK_TILE,),
        in_specs=[pl.BlockSpec((TOK_TILE, HGRN_WIDTH), row),
                  pl.BlockSpec((TOK_TILE, ATTN_WIDTH), row),
                  pl.BlockSpec((TOK_TILE, D_MODEL), row),
                  pl.BlockSpec((None, D_MODEL, D_MODEL), lambda i: (layer, 0, 0)),
                  pl.BlockSpec((1, D_MODEL), const),
                  pl.BlockSpec((D_MODEL, 2 * LANES), const)],
        out_specs=[pl.BlockSpec((TOK_TILE, D_MODEL), row),
                   pl.BlockSpec((TOK_TILE * ROW_TILE, LANES), row),
                   pl.BlockSpec((TOK_TILE, LANES), row),
                   pl.BlockSpec((1, LANES), const)],
        out_shape=[jax.ShapeDtypeStruct((T, D_MODEL), F32),
                   jax.ShapeDtypeStruct((T * ROW_TILE, LANES), jnp.uint32),
                   jax.ShapeDtypeStruct((T, LANES), F32),
                   jax.ShapeDtypeStruct((1, LANES), F32)],
        scratch_shapes=[pltpu.VMEM((1, LANES), F32), pltpu.VMEM((D_MODEL, D_MODEL), BF16)],
        compiler_params=_cparams(("arbitrary",)),
        name="outproj_router",
    )(y_rec, y_att, x2d, w_out, nw, w_router)


def _row_copy(src_ref, src_row, dst_ref, dst_row, sem):
    return pltpu.make_async_copy(src_ref.at[pl.ds(src_row, 1), :],
                                 dst_ref.at[pl.ds(dst_row, 1), :], sem)


assert ROW_TILE == SUBLANES


def _lane_block_rows(c, n_rows):
    return pl.ds(c, n_rows, stride=ROW_TILE)


def _tile_row_copy(src_ref, src_row, dst_ref, dst_row, sem):
    return pltpu.make_async_copy(
        src_ref.at[pl.ds(pl.multiple_of(src_row * ROW_TILE, ROW_TILE), ROW_TILE), :],
        dst_ref.at[pl.ds(pl.multiple_of(dst_row * ROW_TILE, ROW_TILE), ROW_TILE), :], sem)


def _dispatch_kernel(pos_ref, h_ref, hs_init_ref, hs_ref, sem):
    del hs_init_ref

    def issue(t, carry):
        _tile_row_copy(h_ref, t, hs_ref, pos_ref[t], sem).start()
        return carry

    lax.fori_loop(0, DISP_TILE, issue, 0, unroll=8)

    pltpu.make_async_copy(h_ref, hs_ref.at[pl.ds(0, DISP_TILE * ROW_TILE), :], sem).wait()


def _dispatch(pos, h2, n_rows):
    T = h2.shape[0] // ROW_TILE
    return pl.pallas_call(
        _dispatch_kernel,
        grid=(T // DISP_TILE,),
        in_specs=[pl.BlockSpec((DISP_TILE,), lambda i: (i,), memory_space=pltpu.SMEM),
                  pl.BlockSpec((DISP_TILE * ROW_TILE, LANES), lambda i: (i, 0)),
                  pl.BlockSpec(memory_space=pl.ANY)],
        out_specs=pl.BlockSpec(memory_space=pl.ANY),
        out_shape=jax.ShapeDtypeStruct((n_rows * ROW_TILE, LANES), jnp.uint32),
        scratch_shapes=[pltpu.SemaphoreType.DMA(())],
        input_output_aliases={2: 0},
        compiler_params=_cparams(("arbitrary",)),
        name="moe_dispatch",
    )(pos, h2, jnp.zeros((n_rows * ROW_TILE, LANES), jnp.uint32))


def _expert_kernel(tg_ref, nv_ref, hs_ref, w1_ref, w3_ref, w2_ref, ys_ref,
                   hb_ref, cw_ref, acc_ref):
    j = pl.program_id(0)
    e = pl.program_id(1)
    valid = j < nv_ref[0]

    @pl.when(valid & (e == 0))
    def _():
        half = D_MODEL // 2
        for c in range(H_BLOCKS):
            pairs = hs_ref[_lane_block_rows(c, MOE_ROWS), :]
            for part in range(2):
                vals = pltpu.unpack_elementwise(pairs, index=part, packed_dtype=BF16,
                                                unpacked_dtype=F32)
                lo = part * half + c * LANES
                hb_ref[:, lo:lo + LANES] = vals.astype(BF16)
        rec = pltpu.bitcast(hs_ref[_lane_block_rows(REC_BLOCK, MOE_ROWS), :], F32)
        lane = lax.broadcasted_iota(jnp.int32, rec.shape, 1)

        def field(slot):
            return jnp.sum(jnp.where(lane == slot, rec, 0.0), axis=-1, keepdims=True)

        first = (tg_ref[j] * EXPERTS_PER_GROUP).astype(F32)
        lanef = lane.astype(F32)
        cw_ref[...] = (jnp.where(field(R_E1) - first == lanef, field(R_G1), 0.0)
                       + jnp.where(field(R_E2) - first == lanef, field(R_G2), 0.0))

    @pl.when(valid)
    def _():
        h = hb_ref[...]
        cw = cw_ref[...]
        lane = lax.broadcasted_iota(jnp.int32, cw.shape, 1)
        hid = []
        for k in range(MOE_EPS):
            gate = jnp.sum(jnp.where(lane == e * MOE_EPS + k, cw, 0.0), axis=-1, keepdims=True)
            a = jnp.dot(h, w1_ref[k].astype(BF16), preferred_element_type=F32)
            b = jnp.dot(h, w3_ref[k].astype(BF16), preferred_element_type=F32)
            hid.append((_silu(a) * b * gate).astype(BF16))
        w2 = w2_ref[...].astype(BF16).reshape(MOE_EPS * EXPERT_FF, D_MODEL)
        y = jnp.dot(jnp.concatenate(hid, axis=-1), w2, preferred_element_type=F32)

        last = pl.num_programs(1) - 1

        @pl.when(e == 0)
        def _():
            acc_ref[...] = y

        @pl.when((e > 0) & (e < last))
        def _():
            acc_ref[...] += y

        @pl.when(e == last)
        def _():
            total = acc_ref[...] + y
            for c in range(ROW_TILE):
                ys_ref[_lane_block_rows(c, MOE_ROWS), :] = total[:, c * LANES:(c + 1) * LANES]

    @pl.when(jnp.logical_not(valid) & (e == 0))
    def _():
        ys_ref[...] = jnp.zeros_like(ys_ref)


def _experts(tile_group, n_valid, hs, w1, w3, w2, layer):
    n_rows = hs.shape[0] // ROW_TILE
    rows_in = lambda j, e, tg, nv: (jnp.minimum(j, nv[0] - 1), 0)
    steps = EXPERTS_PER_GROUP // MOE_EPS
    assert steps >= 2

    def wsel(j, e, tg, nv):
        last = nv[0] - 1
        blk = jnp.where(j <= last, tg[j] * steps + e, tg[last] * steps + steps - 1)
        return (layer, blk, 0, 0)

    return pl.pallas_call(
        _expert_kernel,
        grid_spec=pltpu.PrefetchScalarGridSpec(
            num_scalar_prefetch=2,
            grid=(n_rows // MOE_ROWS, steps),
            in_specs=[pl.BlockSpec((MOE_ROWS * ROW_TILE, LANES), rows_in),
                      pl.BlockSpec((None, MOE_EPS, D_MODEL, EXPERT_FF), wsel),
                      pl.BlockSpec((None, MOE_EPS, D_MODEL, EXPERT_FF), wsel),
                      pl.BlockSpec((None, MOE_EPS, EXPERT_FF, D_MODEL), wsel)],
            out_specs=pl.BlockSpec((MOE_ROWS * ROW_TILE, LANES), lambda j, e, tg, nv: (j, 0)),
            scratch_shapes=[pltpu.VMEM((MOE_ROWS, D_MODEL), BF16),
                            pltpu.VMEM((MOE_ROWS, LANES), F32),
                            pltpu.VMEM((MOE_ROWS, D_MODEL), F32)]),
        out_shape=jax.ShapeDtypeStruct((n_rows * ROW_TILE, LANES), F32),
        compiler_params=_cparams(("arbitrary", "arbitrary")),
        name="moe_experts",
    )(tile_group, n_valid, hs, w1, w3, w2)


def _combine_kernel(pos_ref, x1_ref, fw_ref, ys_ref, o_ref, buf_ref, sem, *, final_norm):
    i = pl.program_id(0)
    n = pl.num_programs(0)

    def gather(tile, slot, start):
        if not start:
            pltpu.make_async_copy(ys_ref.at[pl.ds(0, COMB_TILE * ROW_TILE), :],
                                  buf_ref.at[slot], sem.at[slot]).wait()
            return
        base = tile * COMB_TILE

        def body(t, carry):
            _tile_row_copy(ys_ref, pos_ref[base + t], buf_ref.at[slot], t, sem.at[slot]).start()
            return carry

        lax.fori_loop(0, COMB_TILE, body, 0, unroll=8)

    @pl.when(i == 0)
    def _():
        gather(0, 0, True)

    for slot in range(2):
        @pl.when((i + 1 < n) & ((i + 1) % 2 == slot))
        def _():
            gather(i + 1, slot, True)

    for slot in range(2):
        @pl.when(i % 2 == slot)
        def _():
            gather(i, slot, False)
            y = jnp.concatenate([buf_ref[slot, _lane_block_rows(c, COMB_TILE), :]
                                 for c in range(ROW_TILE)], axis=-1)
            x2 = x1_ref[...] + y
            if final_norm:
                x2 = x2 * lax.rsqrt(jnp.mean(x2 * x2, axis=-1, keepdims=True) + EPS) * fw_ref[...]
            o_ref[...] = x2


def _combine(pos, x1, final_w, ys, final_norm):
    T = x1.shape[0]
    row = lambda i: (i, 0)
    return pl.pallas_call(
        functools.partial(_combine_kernel, final_norm=final_norm),
        grid=(T // COMB_TILE,),
        in_specs=[pl.BlockSpec(memory_space=pltpu.SMEM),
                  pl.BlockSpec((COMB_TILE, D_MODEL), row),
                  pl.BlockSpec((1, D_MODEL), lambda i: (0, 0)),
                  pl.BlockSpec(memory_space=pl.ANY)],
        out_specs=pl.BlockSpec((COMB_TILE, D_MODEL), row),
        out_shape=jax.ShapeDtypeStruct((T, D_MODEL), F32),
        scratch_shapes=[pltpu.VMEM((2, COMB_TILE * ROW_TILE, LANES), F32),
                        pltpu.SemaphoreType.DMA((2,))],
        compiler_params=_cparams(("arbitrary",)),
        name="moe_combine",
    )(pos, x1, final_w, ys)


def _sorted_layout(route, counts, n_tiles):
    cnt = counts[0, :N_GROUPS].astype(jnp.int32)
    padded = (cnt + MOE_ROWS - 1) // MOE_ROWS * MOE_ROWS
    ends = jnp.cumsum(padded)
    offs = ends - padded
    tile_start = jnp.arange(n_tiles, dtype=jnp.int32) * MOE_ROWS
    tile_group = jnp.minimum(
        jnp.sum(ends[None, :] <= tile_start[:, None], axis=1), N_GROUPS - 1).astype(jnp.int32)
    n_valid = (ends[-1:] // MOE_ROWS).astype(jnp.int32)
    group = route[:, R_GROUP].astype(jnp.int32)
    ids = jnp.arange(N_GROUPS, dtype=jnp.int32)
    seg = jnp.sum(jnp.where(group[:, None] == ids[None, :], offs[None, :], 0), axis=1)
    return seg + route[:, R_RANK].astype(jnp.int32), tile_group, n_valid


def kernel(x, norm_mix_w, w_in, hgrn_lower_bounds, hgrn_gnorm_w, attn_rel_bias, attn_norm_w,
           w_out, norm_ffn_w, moe_w_group, moe_w_expert, moe_w1, moe_w3, moe_w2, final_norm_w):
    B, S, D = x.shape
    T = B * S
    lb_all = jnp.cumsum(jax.nn.softmax(hgrn_lower_bounds.astype(F32), axis=0), axis=0)
    lb_all = lb_all - lb_all[0]
    log_lb = jnp.log(lb_all)
    log_1m_lb = jnp.log1p(-lb_all)

    xt = x.reshape(T, D)
    for l in range(DEPTH):
        q, lf, bc, v, g, aq, ak, av = _norm_inproj(
            xt, norm_mix_w[l][None], w_in, l, log_lb[l][None], log_1m_lb[l][None])
        y_rec = _hgrn(q, lf, bc, v, g, hgrn_gnorm_w[l][None], B, S)
        y_att = _attention(aq, ak, av, _bias_table(attn_rel_bias[l]), attn_norm_w[l][None], B, S)
        w_r = jnp.concatenate(
            [moe_w_expert[l], moe_w_group[l],
             jnp.zeros((D, LANES - N_EXPERTS - N_GROUPS), F32)], axis=1)
        w_r_hi = w_r.astype(BF16)
        w_router = jnp.concatenate([w_r_hi, (w_r - w_r_hi.astype(F32)).astype(BF16)], axis=1)
        x1, h2, route, counts = _outproj_router(y_rec, y_att, xt, w_out, l,
                                                norm_ffn_w[l][None], w_router)
        n_tiles = T // MOE_ROWS + N_GROUPS
        pos, tile_group, n_valid = _sorted_layout(route, counts, n_tiles)
        hs = _dispatch(pos, h2, n_tiles * MOE_ROWS)
        ys = _experts(tile_group, n_valid, hs, moe_w1, moe_w3, moe_w2, l)
        xt = _combine(pos, x1, final_norm_w[None], ys, final_norm=(l == DEPTH - 1))
    return xt.reshape(B, S, D)
```

```python
import functools
import math

import jax
import jax.numpy as jnp
from jax import lax
from jax.experimental import pallas as pl
from jax.experimental.pallas import tpu as pltpu

F32 = jnp.float32
BF16 = jnp.bfloat16

D_MODEL = 1024
DEPTH = 2
CHUNK = 64
HGRN_WIDTH = 512
HGRN_HEADS = 4
HGRN_D = 128
ATTN_WIDTH = 512
ATTN_HEADS = 8
ATTN_HEAD_DIM = 64
LEFT_CHUNKS = 8
MAX_REL = 128
N_GROUPS = 4
EXPERTS_PER_GROUP = 8
N_EXPERTS = 32
EXPERT_FF = 256
IN_COLS = 4 * HGRN_WIDTH + 3 * ATTN_WIDTH
EPS = 1e-6

LANES = 128
VMEM_LIMIT = 56 * 1024 * 1024
NEG_BIG = -1e30
LOG2_E = 1.4426950408889634
ATTN_MIN_DENOM = 2.0 ** -90

TOK_TILE = 512
ATTN_QB = 256
ATTN_KB = 3 * ATTN_QB
HGRN_TB = 512
MOE_ROWS = 1024
MOE_EPS = 4
ROW_W = D_MODEL + LANES
DISP_TILE = 1024
COMB_TILE = 512
R_E1, R_E2, R_G1, R_G2, R_GROUP, R_RANK = range(6)


def _cparams(sem):
    return pltpu.CompilerParams(dimension_semantics=sem, vmem_limit_bytes=VMEM_LIMIT)


def _silu(x):
    return x * jax.nn.sigmoid(x)


def _chunk_cumsum(x, row_in_chunk):
    acc = x
    step = 1
    while step < CHUNK:
        acc = acc + jnp.where(row_in_chunk >= step, pltpu.roll(acc, step, 0), 0.0)
        step *= 2
    return acc


def _norm_inproj_kernel(x_ref, nw_ref, w_ref, la_ref, lc_ref,
                        q_ref, lf_ref, bc_ref, v_ref, g_ref, aq_ref, ak_ref, av_ref, wbf_ref):
    @pl.when(pl.program_id(0) == 0)
    def _():
        wbf_ref[...] = w_ref[...].astype(BF16)

    x = x_ref[...]
    ms = jnp.mean(x * x, axis=-1, keepdims=True)
    h = (x * lax.rsqrt(ms + EPS) * nw_ref[...]).astype(BF16)

    def proj(j):
        return jnp.dot(h, wbf_ref[:, j * 512:(j + 1) * 512], preferred_element_type=F32)

    q_ref[...] = _silu(proj(0)).astype(BF16)
    z = proj(1)
    log_sig = jnp.minimum(z, 0.0) - jnp.log(1.0 + jnp.exp(-jnp.abs(z)))
    a = la_ref[...]
    b = lc_ref[...] + log_sig
    lf2 = (jnp.maximum(a, b) + jnp.log(1.0 + jnp.exp(-jnp.abs(a - b)))) * LOG2_E
    lf_ref[...] = lf2
    row_in_chunk = lax.broadcasted_iota(jnp.int32, lf2.shape, 0) & (CHUNK - 1)
    bc_ref[...] = _chunk_cumsum(lf2, row_in_chunk)
    v_ref[...] = proj(2).astype(BF16)
    g_ref[...] = _silu(proj(3)).astype(BF16)
    aq_ref[...] = (proj(4) * (LOG2_E / math.sqrt(ATTN_HEAD_DIM))).astype(BF16)
    ak_ref[...] = proj(5).astype(BF16)
    av_ref[...] = proj(6).astype(BF16)


def _norm_inproj(x2d, nw, w_in, layer, log_lb, log_1m_lb):
    T = x2d.shape[0]
    row = lambda i: (i, 0)
    const = lambda i: (0, 0)
    out_bf = jax.ShapeDtypeStruct((T, 512), BF16)
    out_f32 = jax.ShapeDtypeStruct((T, 512), F32)
    blk = pl.BlockSpec((TOK_TILE, 512), row)
    return pl.pallas_call(
        _norm_inproj_kernel,
        grid=(T // TOK_TILE,),
        in_specs=[pl.BlockSpec((TOK_TILE, D_MODEL), row),
                  pl.BlockSpec((1, D_MODEL), const),
                  pl.BlockSpec((None, D_MODEL, IN_COLS), lambda i: (layer, 0, 0),
                               pipeline_mode=pl.Buffered(1)),
                  pl.BlockSpec((1, 512), const),
                  pl.BlockSpec((1, 512), const)],
        out_specs=[blk] * 8,
        out_shape=[out_bf, out_f32, out_f32, out_bf, out_bf, out_bf, out_bf, out_bf],
        scratch_shapes=[pltpu.VMEM((D_MODEL, IN_COLS), BF16)],
        compiler_params=_cparams(("arbitrary",)),
        name="norm_inproj",
    )(x2d, nw, w_in, log_lb, log_1m_lb)


def _level_ref(bcum, half):
    if half >= 8:
        b3 = bcum.reshape(CHUNK // (2 * half), 2 * half, HGRN_D)
        ref = jnp.broadcast_to(b3[:, half - 1:half, :], b3.shape)
        return ref.reshape(CHUNK, HGRN_D)
    b3 = bcum.reshape(CHUNK // 8, 8, HGRN_D)
    if half == 4:
        ref = jnp.broadcast_to(b3[:, 3:4, :], b3.shape)
    else:
        sub = lax.broadcasted_iota(jnp.int32, b3.shape, 1)
        ref = jnp.where(sub < 4,
                        jnp.broadcast_to(b3[:, 1:2, :], b3.shape),
                        jnp.broadcast_to(b3[:, 5:6, :], b3.shape))
    return ref.reshape(CHUNK, HGRN_D)


def _dot_nt(a, b):
    return lax.dot_general(a, b, (((1,), (1,)), ((), ())), preferred_element_type=F32)


def _dot_tn(a, b):
    return lax.dot_general(a, b, (((0,), (0,)), ((), ())), preferred_element_type=F32)


HGRN_LEVELS = (32, 16, 8, 4, 2)


def _hgrn_kernel(q_ref, lf_ref, bc_ref, v_ref, g_ref, gw_ref, o_ref, *st_refs):
    @pl.when(pl.program_id(1) == 0)
    def _():
        for st_ref in st_refs:
            st_ref[...] = jnp.zeros_like(st_ref)

    row = lax.broadcasted_iota(jnp.int32, (CHUNK, HGRN_D), 0)
    sign = {half: jnp.where((row & half) != 0, 1.0, -1.0) for half in HGRN_LEVELS}
    r64 = lax.broadcasted_iota(jnp.int32, (CHUNK, CHUNK), 0)
    c64 = lax.broadcasted_iota(jnp.int32, (CHUNK, CHUNK), 1)
    differ = jnp.where(r64 > c64, r64 ^ c64, 0)
    level = sum((differ >= half).astype(jnp.int32) for half in (1,) + HGRN_LEVELS)

    def chunk_body(c, carry):
        rows = pl.ds(pl.multiple_of(c * CHUNK, CHUNK), CHUNK)
        heads = range(HGRN_HEADS)
        cols = [slice(h * HGRN_D, (h + 1) * HGRN_D) for h in heads]
        q = [q_ref[rows, cs].astype(F32) for cs in cols]
        f = [jnp.exp2(lf_ref[rows, cs]) for cs in cols]
        k = [1.0 - x for x in f]
        bcum = [bc_ref[rows, cs] for cs in cols]

        scores = [_dot_nt((q[h] * f[h]).astype(BF16), k[h].astype(BF16)) for h in heads]
        scores = [jnp.where(level == 1, s, 0.0) for s in scores]
        for half in HGRN_LEVELS:
            lvl = int(math.log2(half)) + 1
            for h in heads:
                fac = jnp.exp2((bcum[h] - _level_ref(bcum[h], half)) * sign[half])
                s_l = _dot_nt((q[h] * fac).astype(BF16), (k[h] * fac).astype(BF16))
                scores[h] = jnp.where(level == lvl, s_l, scores[h])

        v = [v_ref[rows, cs] for cs in cols]
        st = [st_ref[...] for st_ref in st_refs]
        inter = [_dot_nt((q[h] * jnp.exp2(bcum[h])).astype(BF16), st[h].astype(BF16))
                 for h in heads]
        intra = [jnp.dot(scores[h].astype(BF16), v[h], preferred_element_type=F32)
                 for h in heads]
        for h in heads:
            b_last = bcum[h][CHUNK - 1:CHUNK, :]
            k_dec = (k[h] * jnp.exp2(b_last - bcum[h])).astype(BF16)
            st_refs[h][...] = st[h] * jnp.exp2(b_last) + _dot_tn(v[h], k_dec)
        for h in heads:
            diag = jnp.sum(q[h] * k[h], axis=-1, keepdims=True) * v[h].astype(F32)
            o = intra[h] + diag + inter[h]
            o = o * lax.rsqrt(jnp.mean(o * o, axis=-1, keepdims=True) + EPS)
            o = o * gw_ref[:, cols[h]] * g_ref[rows, cols[h]].astype(F32)
            o_ref[rows, cols[h]] = o.astype(BF16)
        return carry

    lax.fori_loop(0, HGRN_TB // CHUNK, chunk_body, 0, unroll=2)


def _hgrn(q, lf, bc, v, g, gnorm_w, batch, seq):
    blk = pl.BlockSpec((HGRN_TB, HGRN_WIDTH), lambda b, i: (b * (seq // HGRN_TB) + i, 0))
    return pl.pallas_call(
        _hgrn_kernel,
        grid=(batch, seq // HGRN_TB),
        in_specs=[blk] * 5 + [pl.BlockSpec((1, HGRN_WIDTH), lambda b, i: (0, 0))],
        out_specs=blk,
        out_shape=jax.ShapeDtypeStruct(q.shape, BF16),
        scratch_shapes=[pltpu.VMEM((HGRN_D, HGRN_D), F32)] * HGRN_HEADS,
        compiler_params=_cparams(("parallel", "arbitrary")),
        name="hgrn2",
    )(q, lf, bc, v, g, gnorm_w)


def _attn_kernel(bmax_ref, q_ref, k0_ref, k1_ref, k2_ref, v0_ref, v1_ref, v2_ref, bias_ref,
                 nw_ref, o_ref):
    i = pl.program_id(1)
    lane = lax.broadcasted_iota(jnp.int32, (1, LANES), 1)
    lo = lane < ATTN_HEAD_DIM

    def block(exact_max, pad_mask):
        den_min = None
        for pair in range(ATTN_HEADS // 2):
            cols = slice(pair * LANES, (pair + 1) * LANES)
            qp = q_ref[:, cols]
            kp = jnp.concatenate([k0_ref[:, cols], k1_ref[:, cols], k2_ref[:, cols]], axis=0)
            vp = jnp.concatenate([v0_ref[:, cols], v1_ref[:, cols], v2_ref[:, cols]], axis=0)
            if not exact_max:
                q_norm = jnp.sqrt(jnp.sum(jnp.square(qp.astype(F32)), axis=-1, keepdims=True))
                k_norm = jnp.sqrt(jnp.max(
                    jnp.sum(jnp.square(kp.astype(F32)), axis=-1, keepdims=True),
                    axis=0, keepdims=True))
                qk_bound = q_norm * k_norm
            outs = []
            for hh in range(2):
                head = 2 * pair + hh
                mine = lo if hh == 0 else ~lo
                qh = jnp.where(mine, qp, jnp.zeros_like(qp))
                s = _dot_nt(qh, kp) + bias_ref[head]
                if pad_mask is not None:
                    s = s + pad_mask
                if exact_max:
                    m = jnp.max(s, axis=-1, keepdims=True)
                else:
                    m = qk_bound + bmax_ref[head]
                e = jnp.exp2(s - m)
                pv = jnp.dot(e.astype(BF16), jnp.where(mine, vp, jnp.ones_like(vp)),
                             preferred_element_type=F32)
                outs.append(pv / pltpu.roll(pv, ATTN_HEAD_DIM, 1))
                if not exact_max:
                    den = jnp.where(mine, jnp.inf, pv)
                    den_min = den if den_min is None else jnp.minimum(den_min, den)
            o = jnp.where(lo, outs[0], outs[1])
            o2 = o * o
            ms0 = jnp.sum(jnp.where(lo, o2, 0.0), axis=-1, keepdims=True)
            ms1 = jnp.sum(jnp.where(lo, 0.0, o2), axis=-1, keepdims=True)
            ms = jnp.where(lo, ms0, ms1) * (1.0 / ATTN_HEAD_DIM)
            o_ref[:, cols] = (o * lax.rsqrt(ms + EPS) * nw_ref[:, cols]).astype(BF16)
        return den_min

    def run(pad_mask):
        den_min = block(False, pad_mask)
        safe = jnp.min(den_min) >= ATTN_MIN_DENOM

        @pl.when(jnp.logical_not(safe))
        def _():
            block(True, pad_mask)

    @pl.when(i >= 2)
    def _():
        run(None)

    @pl.when(i < 2)
    def _():
        col = lax.broadcasted_iota(jnp.int32, (1, ATTN_KB), 1)
        run(jnp.where(col >= (2 - i) * ATTN_QB, 0.0, NEG_BIG))


def _attention(aq, ak, av, bias_tab, bias_max, norm_w, batch, seq):
    nb = seq // ATTN_QB
    qblk = pl.BlockSpec((ATTN_QB, ATTN_WIDTH), lambda b, i: (b * nb + i, 0))

    def kblk(back):
        return pl.BlockSpec((ATTN_QB, ATTN_WIDTH),
                            lambda b, i: (b * nb + jnp.maximum(i - back, 0), 0))

    return pl.pallas_call(
        _attn_kernel,
        grid=(batch, nb),
        in_specs=[pl.BlockSpec(memory_space=pltpu.SMEM),
                  qblk, kblk(2), kblk(1), kblk(0), kblk(2), kblk(1), kblk(0),
                  pl.BlockSpec((ATTN_HEADS, ATTN_QB, ATTN_KB), lambda b, i: (0, 0, 0)),
                  pl.BlockSpec((1, ATTN_WIDTH), lambda b, i: (0, 0))],
        out_specs=qblk,
        out_shape=jax.ShapeDtypeStruct(aq.shape, BF16),
        compiler_params=_cparams(("parallel", "arbitrary")),
        name="band_attn",
    )(bias_max, aq, ak, ak, ak, av, av, av, bias_tab, norm_w)


BIAS_ROW = ATTN_QB + ATTN_KB


def _bias_kernel(base_ref, o_ref):
    row = lax.broadcasted_iota(jnp.int32, (ATTN_QB, BIAS_ROW), 0)
    x = jnp.broadcast_to(base_ref[0], (ATTN_QB, BIAS_ROW))
    x = pltpu.roll(x, BIAS_ROW - ATTN_QB, 1)
    for bit in range(int(math.log2(ATTN_QB))):
        x = jnp.where(((row >> bit) & 1) == 1, pltpu.roll(x, 1 << bit, 1), x)
    r = lax.broadcasted_iota(jnp.int32, (ATTN_QB, ATTN_KB), 0)
    c = lax.broadcasted_iota(jnp.int32, (ATTN_QB, ATTN_KB), 1)
    dchunk = (2 * ATTN_QB) // CHUNK + r // CHUNK - c // CHUNK
    valid = (dchunk >= 0) & (dchunk <= LEFT_CHUNKS)
    o_ref[0] = jnp.where(valid, x[:, :ATTN_KB] * LOG2_E, NEG_BIG)


def _bias_table(rel_bias):
    H = rel_bias.shape[0]
    n_hi = 2 * ATTN_QB - MAX_REL + ATTN_QB
    n_lo = BIAS_ROW - n_hi - (2 * MAX_REL + 1)
    rb = rel_bias.astype(F32)
    base = jnp.concatenate([jnp.broadcast_to(rb[:, -1:], (H, n_hi)), rb[:, ::-1],
                            jnp.broadcast_to(rb[:, :1], (H, n_lo))], axis=1)
    return pl.pallas_call(
        _bias_kernel,
        grid=(H,),
        in_specs=[pl.BlockSpec((1, 1, BIAS_ROW), lambda h: (h, 0, 0))],
        out_specs=pl.BlockSpec((1, ATTN_QB, ATTN_KB), lambda h: (h, 0, 0)),
        out_shape=jax.ShapeDtypeStruct((H, ATTN_QB, ATTN_KB), F32),
        compiler_params=_cparams(("parallel",)),
        name="rel_bias_table",
    )(base.reshape(H, 1, BIAS_ROW))


def _outproj_router_kernel(yr_ref, ya_ref, x_ref, wo_ref, nw_ref, wr_ref,
                           x1_ref, h2_ref, route_ref, cnt_ref, run_ref, wob_ref):
    @pl.when(pl.program_id(0) == 0)
    def _():
        run_ref[...] = jnp.zeros_like(run_ref)
        wob_ref[...] = wo_ref[...].astype(BF16)

    acc = jnp.dot(yr_ref[...], wob_ref[0:HGRN_WIDTH, :], preferred_element_type=F32)
    acc = acc + jnp.dot(ya_ref[...], wob_ref[HGRN_WIDTH:, :], preferred_element_type=F32)
    x1 = x_ref[...] + acc
    x1_ref[...] = x1
    h = x1 * lax.rsqrt(jnp.mean(x1 * x1, axis=-1, keepdims=True) + EPS) * nw_ref[...]
    h_hi = h.astype(BF16)
    h2_ref[:, :D_MODEL] = h_hi.astype(F32)

    h_lo = (h - h_hi.astype(F32)).astype(BF16)
    hi_prod = jnp.dot(h_hi, wr_ref[...], preferred_element_type=F32)
    lg = (hi_prod[:, :LANES] + hi_prod[:, LANES:]
          + jnp.dot(h_lo, wr_ref[:, :LANES], preferred_element_type=F32))
    lane = lax.broadcasted_iota(jnp.int32, lg.shape, 1)
    lanef = lane.astype(F32)
    is_g = (lane >= N_EXPERTS) & (lane < N_EXPERTS + N_GROUPS)
    g_max = jnp.max(jnp.where(is_g, lg, -jnp.inf), axis=-1, keepdims=True)
    g_sum = jnp.sum(jnp.where(is_g, jnp.exp(lg - g_max), 0.0), axis=-1, keepdims=True)
    g_gate = 1.0 / g_sum
    g_idx = jnp.min(jnp.where(is_g & (lg == g_max), lanef - N_EXPERTS, 1e9),
                    axis=-1, keepdims=True)
    in_grp = (lane < N_EXPERTS) & ((lane // EXPERTS_PER_GROUP).astype(F32) == g_idx)
    v1 = jnp.max(jnp.where(in_grp, lg, -jnp.inf), axis=-1, keepdims=True)
    i1 = jnp.min(jnp.where(in_grp & (lg == v1), lanef, 1e9), axis=-1, keepdims=True)
    rest = in_grp & (lanef != i1)
    v2 = jnp.max(jnp.where(rest, lg, -jnp.inf), axis=-1, keepdims=True)
    i2 = jnp.min(jnp.where(rest & (lg == v2), lanef, 1e9), axis=-1, keepdims=True)
    e2 = jnp.exp(v2 - v1)
    gate1 = g_gate / (1.0 + e2)
    gate2 = gate1 * e2

    hit = lanef == g_idx
    onehot = jnp.where(hit, 1.0, 0.0)
    n = lg.shape[0]
    rr = lax.broadcasted_iota(jnp.int32, (n, n), 0)
    cc = lax.broadcasted_iota(jnp.int32, (n, n), 1)
    earlier = jnp.where(cc < rr, 1.0, 0.0).astype(BF16)
    before = jnp.dot(earlier, onehot.astype(BF16), preferred_element_type=F32) + run_ref[...]
    rank = jnp.sum(jnp.where(hit, before, 0.0), axis=-1, keepdims=True)
    run_ref[...] += jnp.sum(onehot, axis=0, keepdims=True)
    cnt_ref[...] = run_ref[...]

    rec = jnp.zeros_like(lg)
    for slot, val in ((R_E1, i1), (R_E2, i2), (R_G1, gate1), (R_G2, gate2),
                      (R_GROUP, g_idx), (R_RANK, rank)):
        rec = jnp.where(lane == slot, val, rec)
    route_ref[...] = rec
    h2_ref[:, D_MODEL:] = rec


def _outproj_router(y_rec, y_att, x2d, w_out, layer, nw, w_router):
    T = x2d.shape[0]
    row = lambda i: (i, 0)
    const = lambda i: (0, 0)
    return pl.pallas_call(
        _outproj_router_kernel,
        grid=(T // TOK_TILE,),
        in_specs=[pl.BlockSpec((TOK_TILE, HGRN_WIDTH), row),
                  pl.BlockSpec((TOK_TILE, ATTN_WIDTH), row),
                  pl.BlockSpec((TOK_TILE, D_MODEL), row),
                  pl.BlockSpec((None, D_MODEL, D_MODEL), lambda i: (layer, 0, 0)),
                  pl.BlockSpec((1, D_MODEL), const),
                  pl.BlockSpec((D_MODEL, 2 * LANES), const)],
        out_specs=[pl.BlockSpec((TOK_TILE, D_MODEL), row),
                   pl.BlockSpec((TOK_TILE, ROW_W), row),
                   pl.BlockSpec((TOK_TILE, LANES), row),
                   pl.BlockSpec((1, LANES), const)],
        out_shape=[jax.ShapeDtypeStruct((T, D_MODEL), F32),
                   jax.ShapeDtypeStruct((T, ROW_W), F32),
                   jax.ShapeDtypeStruct((T, LANES), F32),
                   jax.ShapeDtypeStruct((1, LANES), F32)],
        scratch_shapes=[pltpu.VMEM((1, LANES), F32), pltpu.VMEM((D_MODEL, D_MODEL), BF16)],
        compiler_params=_cparams(("arbitrary",)),
        name="outproj_router",
    )(y_rec, y_att, x2d, w_out, nw, w_router)


def _row_copy(src_ref, src_row, dst_ref, dst_row, sem):
    return pltpu.make_async_copy(src_ref.at[pl.ds(src_row, 1), :],
                                 dst_ref.at[pl.ds(dst_row, 1), :], sem)


def _dispatch_kernel(pos_ref, h_ref, hs_init_ref, hs_ref, sem):
    del hs_init_ref

    def issue(t, carry):
        _row_copy(h_ref, t, hs_ref, pos_ref[t], sem).start()
        return carry

    lax.fori_loop(0, DISP_TILE, issue, 0, unroll=8)

    pltpu.make_async_copy(h_ref, hs_ref.at[pl.ds(0, DISP_TILE), :], sem).wait()


def _dispatch(pos, h2, n_rows):
    T = h2.shape[0]
    return pl.pallas_call(
        _dispatch_kernel,
        grid=(T // DISP_TILE,),
        in_specs=[pl.BlockSpec((DISP_TILE,), lambda i: (i,), memory_space=pltpu.SMEM),
                  pl.BlockSpec((DISP_TILE, ROW_W), lambda i: (i, 0)),
                  pl.BlockSpec(memory_space=pl.ANY)],
        out_specs=pl.BlockSpec(memory_space=pl.ANY),
        out_shape=jax.ShapeDtypeStruct((n_rows, ROW_W), F32),
        scratch_shapes=[pltpu.SemaphoreType.DMA(())],
        input_output_aliases={2: 0},
        compiler_params=_cparams(("arbitrary",)),
        name="moe_dispatch",
    )(pos, h2, jnp.zeros((n_rows, ROW_W), F32))


def _expert_kernel(tg_ref, nv_ref, hs_ref, w1_ref, w3_ref, w2_ref, ys_ref, hb_ref, cw_ref):
    j = pl.program_id(0)
    e = pl.program_id(1)
    valid = j < nv_ref[0]

    @pl.when(valid & (e == 0))
    def _():
        hb_ref[...] = hs_ref[:, :D_MODEL].astype(BF16)
        rec = hs_ref[:, D_MODEL:]
        lane = lax.broadcasted_iota(jnp.int32, rec.shape, 1)

        def field(slot):
            return jnp.sum(jnp.where(lane == slot, rec, 0.0), axis=-1, keepdims=True)

        first = (tg_ref[j] * EXPERTS_PER_GROUP).astype(F32)
        lanef = lane.astype(F32)
        cw_ref[...] = (jnp.where(field(R_E1) - first == lanef, field(R_G1), 0.0)
                       + jnp.where(field(R_E2) - first == lanef, field(R_G2), 0.0))

    @pl.when(valid)
    def _():
        h = hb_ref[...]
        cw = cw_ref[...]
        lane = lax.broadcasted_iota(jnp.int32, cw.shape, 1)
        hid = []
        for k in range(MOE_EPS):
            gate = jnp.sum(jnp.where(lane == e * MOE_EPS + k, cw, 0.0), axis=-1, keepdims=True)
            a = jnp.dot(h, w1_ref[k].astype(BF16), preferred_element_type=F32)
            b = jnp.dot(h, w3_ref[k].astype(BF16), preferred_element_type=F32)
            hid.append((_silu(a) * b * gate).astype(BF16))
        w2 = w2_ref[...].astype(BF16).reshape(MOE_EPS * EXPERT_FF, D_MODEL)
        y = jnp.dot(jnp.concatenate(hid, axis=-1), w2, preferred_element_type=F32)

        @pl.when(e == 0)
        def _():
            ys_ref[...] = y

        @pl.when(e > 0)
        def _():
            ys_ref[...] += y

    @pl.when(jnp.logical_not(valid) & (e == 0))
    def _():
        ys_ref[...] = jnp.zeros_like(ys_ref)


def _experts(tile_group, n_valid, hs, w1, w3, w2, layer):
    n_rows = hs.shape[0]
    rows_in = lambda j, e, tg, nv: (jnp.minimum(j, nv[0] - 1), 0)
    steps = EXPERTS_PER_GROUP // MOE_EPS

    def wsel(j, e, tg, nv):
        last = nv[0] - 1
        blk = jnp.where(j <= last, tg[j] * steps + e, tg[last] * steps + steps - 1)
        return (layer, blk, 0, 0)

    return pl.pallas_call(
        _expert_kernel,
        grid_spec=pltpu.PrefetchScalarGridSpec(
            num_scalar_prefetch=2,
            grid=(n_rows // MOE_ROWS, steps),
            in_specs=[pl.BlockSpec((MOE_ROWS, ROW_W), rows_in),
                      pl.BlockSpec((None, MOE_EPS, D_MODEL, EXPERT_FF), wsel),
                      pl.BlockSpec((None, MOE_EPS, D_MODEL, EXPERT_FF), wsel),
                      pl.BlockSpec((None, MOE_EPS, EXPERT_FF, D_MODEL), wsel)],
            out_specs=pl.BlockSpec((MOE_ROWS, D_MODEL), lambda j, e, tg, nv: (j, 0)),
            scratch_shapes=[pltpu.VMEM((MOE_ROWS, D_MODEL), BF16),
                            pltpu.VMEM((MOE_ROWS, LANES), F32)]),
        out_shape=jax.ShapeDtypeStruct((n_rows, D_MODEL), F32),
        compiler_params=_cparams(("arbitrary", "arbitrary")),
        name="moe_experts",
    )(tile_group, n_valid, hs, w1, w3, w2)


def _combine_kernel(pos_ref, x1_ref, fw_ref, ys_ref, o_ref, buf_ref, sem, *, final_norm):
    i = pl.program_id(0)
    n = pl.num_programs(0)

    def gather(tile, slot, start):
        if not start:
            pltpu.make_async_copy(ys_ref.at[pl.ds(0, COMB_TILE), :], buf_ref.at[slot],
                                  sem.at[slot]).wait()
            return
        base = tile * COMB_TILE

        def body(t, carry):
            _row_copy(ys_ref, pos_ref[base + t], buf_ref.at[slot], t, sem.at[slot]).start()
            return carry

        lax.fori_loop(0, COMB_TILE, body, 0, unroll=8)

    @pl.when(i == 0)
    def _():
        gather(0, 0, True)

    for slot in range(2):
        @pl.when((i + 1 < n) & ((i + 1) % 2 == slot))
        def _():
            gather(i + 1, slot, True)

    for slot in range(2):
        @pl.when(i % 2 == slot)
        def _():
            gather(i, slot, False)
            x2 = x1_ref[...] + buf_ref[slot]
            if final_norm:
                x2 = x2 * lax.rsqrt(jnp.mean(x2 * x2, axis=-1, keepdims=True) + EPS) * fw_ref[...]
            o_ref[...] = x2


def _combine(pos, x1, final_w, ys, final_norm):
    T = x1.shape[0]
    row = lambda i: (i, 0)
    return pl.pallas_call(
        functools.partial(_combine_kernel, final_norm=final_norm),
        grid=(T // COMB_TILE,),
        in_specs=[pl.BlockSpec(memory_space=pltpu.SMEM),
                  pl.BlockSpec((COMB_TILE, D_MODEL), row),
                  pl.BlockSpec((1, D_MODEL), lambda i: (0, 0)),
                  pl.BlockSpec(memory_space=pl.ANY)],
        out_specs=pl.BlockSpec((COMB_TILE, D_MODEL), row),
        out_shape=jax.ShapeDtypeStruct((T, D_MODEL), F32),
        scratch_shapes=[pltpu.VMEM((2, COMB_TILE, D_MODEL), F32),
                        pltpu.SemaphoreType.DMA((2,))],
        compiler_params=_cparams(("arbitrary",)),
        name="moe_combine",
    )(pos, x1, final_w, ys)


def _sorted_layout(route, counts, n_tiles):
    cnt = counts[0, :N_GROUPS].astype(jnp.int32)
    padded = (cnt + MOE_ROWS - 1) // MOE_ROWS * MOE_ROWS
    ends = jnp.cumsum(padded)
    offs = ends - padded
    tile_start = jnp.arange(n_tiles, dtype=jnp.int32) * MOE_ROWS
    tile_group = jnp.minimum(
        jnp.sum(ends[None, :] <= tile_start[:, None], axis=1), N_GROUPS - 1).astype(jnp.int32)
    n_valid = (ends[-1:] // MOE_ROWS).astype(jnp.int32)
    group = route[:, R_GROUP].astype(jnp.int32)
    ids = jnp.arange(N_GROUPS, dtype=jnp.int32)
    seg = jnp.sum(jnp.where(group[:, None] == ids[None, :], offs[None, :], 0), axis=1)
    return seg + route[:, R_RANK].astype(jnp.int32), tile_group, n_valid


def kernel(x, norm_mix_w, w_in, hgrn_lower_bounds, hgrn_gnorm_w, attn_rel_bias, attn_norm_w,
           w_out, norm_ffn_w, moe_w_group, moe_w_expert, moe_w1, moe_w3, moe_w2, final_norm_w):
    B, S, D = x.shape
    T = B * S
    lb_all = jnp.cumsum(jax.nn.softmax(hgrn_lower_bounds.astype(F32), axis=0), axis=0)
    lb_all = lb_all - lb_all[0]
    log_lb = jnp.log(lb_all)
    log_1m_lb = jnp.log1p(-lb_all)

    xt = x.reshape(T, D)
    for l in range(DEPTH):
        q, lf, bc, v, g, aq, ak, av = _norm_inproj(
            xt, norm_mix_w[l][None], w_in, l, log_lb[l][None], log_1m_lb[l][None])
        y_rec = _hgrn(q, lf, bc, v, g, hgrn_gnorm_w[l][None], B, S)
        bias_max = jnp.max(attn_rel_bias[l].astype(F32), axis=1) * LOG2_E
        y_att = _attention(aq, ak, av, _bias_table(attn_rel_bias[l]), bias_max,
                           attn_norm_w[l][None], B, S)
        w_r = jnp.concatenate(
            [moe_w_expert[l], moe_w_group[l],
             jnp.zeros((D, LANES - N_EXPERTS - N_GROUPS), F32)], axis=1)
        w_r_hi = w_r.astype(BF16)
        w_router = jnp.concatenate([w_r_hi, (w_r - w_r_hi.astype(F32)).astype(BF16)], axis=1)
        x1, h2, route, counts = _outproj_router(y_rec, y_att, xt, w_out, l,
                                                norm_ffn_w[l][None], w_router)
        n_tiles = T // MOE_ROWS + N_GROUPS
        pos, tile_group, n_valid = _sorted_layout(route, counts, n_tiles)
        hs = _dispatch(pos, h2, n_tiles * MOE_ROWS)
        ys = _experts(tile_group, n_valid, hs, moe_w1, moe_w3, moe_w2, l)
        xt = _combine(pos, x1, final_norm_w[None], ys, final_norm=(l == DEPTH - 1))
    return xt.reshape(B, S, D)
```

```python
import functools
import math

import jax
import jax.numpy as jnp
from jax import lax
from jax.experimental import pallas as pl
from jax.experimental.pallas import tpu as pltpu

F32 = jnp.float32
BF16 = jnp.bfloat16

D_MODEL = 1024
DEPTH = 2
CHUNK = 64
HGRN_WIDTH = 512
HGRN_HEADS = 4
HGRN_D = 128
ATTN_WIDTH = 512
ATTN_HEADS = 8
ATTN_HEAD_DIM = 64
LEFT_CHUNKS = 8
MAX_REL = 128
N_GROUPS = 4
EXPERTS_PER_GROUP = 8
N_EXPERTS = 32
EXPERT_FF = 256
IN_COLS = 4 * HGRN_WIDTH + 3 * ATTN_WIDTH
EPS = 1e-6

LANES = 128
VMEM_LIMIT = 56 * 1024 * 1024
NEG_BIG = -1e30
LOG2_E = 1.4426950408889634
ATTN_MIN_DENOM = 2.0 ** -90

TOK_TILE = 512
ATTN_QB = 256
ATTN_KB = 3 * ATTN_QB
HGRN_TB = 512
MOE_ROWS = 1024
MOE_EPS = 4
ROW_W = D_MODEL + LANES
DISP_TILE = 1024
COMB_TILE = 512
R_E1, R_E2, R_G1, R_G2, R_GROUP, R_RANK = range(6)


def _cparams(sem):
    return pltpu.CompilerParams(dimension_semantics=sem, vmem_limit_bytes=VMEM_LIMIT)


def _silu(x):
    return x * jax.nn.sigmoid(x)


def _chunk_cumsum(x, row_in_chunk):
    acc = x
    step = 1
    while step < CHUNK:
        acc = acc + jnp.where(row_in_chunk >= step, pltpu.roll(acc, step, 0), 0.0)
        step *= 2
    return acc


def _norm_inproj_kernel(x_ref, nw_ref, w_ref, la_ref, lc_ref,
                        q_ref, lf_ref, bc_ref, v_ref, g_ref, aq_ref, ak_ref, av_ref, wbf_ref):
    @pl.when(pl.program_id(0) == 0)
    def _():
        wbf_ref[...] = w_ref[...].astype(BF16)

    x = x_ref[...]
    ms = jnp.mean(x * x, axis=-1, keepdims=True)
    h = (x * lax.rsqrt(ms + EPS) * nw_ref[...]).astype(BF16)

    def proj(j):
        return jnp.dot(h, wbf_ref[:, j * 512:(j + 1) * 512], preferred_element_type=F32)

    q_ref[...] = _silu(proj(0)).astype(BF16)
    z = proj(1)
    log_sig = jnp.minimum(z, 0.0) - jnp.log(1.0 + jnp.exp(-jnp.abs(z)))
    a = la_ref[...]
    b = lc_ref[...] + log_sig
    lf2 = (jnp.maximum(a, b) + jnp.log(1.0 + jnp.exp(-jnp.abs(a - b)))) * LOG2_E
    lf_ref[...] = lf2
    row_in_chunk = lax.broadcasted_iota(jnp.int32, lf2.shape, 0) & (CHUNK - 1)
    bc_ref[...] = _chunk_cumsum(lf2, row_in_chunk)
    v_ref[...] = proj(2).astype(BF16)
    g_ref[...] = _silu(proj(3)).astype(BF16)
    aq_ref[...] = (proj(4) * (LOG2_E / math.sqrt(ATTN_HEAD_DIM))).astype(BF16)
    ak_ref[...] = proj(5).astype(BF16)
    av_ref[...] = proj(6).astype(BF16)


def _norm_inproj(x2d, nw, w_in, layer, log_lb, log_1m_lb):
    T = x2d.shape[0]
    row = lambda i: (i, 0)
    const = lambda i: (0, 0)
    out_bf = jax.ShapeDtypeStruct((T, 512), BF16)
    out_f32 = jax.ShapeDtypeStruct((T, 512), F32)
    blk = pl.BlockSpec((TOK_TILE, 512), row)
    return pl.pallas_call(
        _norm_inproj_kernel,
        grid=(T // TOK_TILE,),
        in_specs=[pl.BlockSpec((TOK_TILE, D_MODEL), row),
                  pl.BlockSpec((1, D_MODEL), const),
                  pl.BlockSpec((None, D_MODEL, IN_COLS), lambda i: (layer, 0, 0),
                               pipeline_mode=pl.Buffered(1)),
                  pl.BlockSpec((1, 512), const),
                  pl.BlockSpec((1, 512), const)],
        out_specs=[blk] * 8,
        out_shape=[out_bf, out_f32, out_f32, out_bf, out_bf, out_bf, out_bf, out_bf],
        scratch_shapes=[pltpu.VMEM((D_MODEL, IN_COLS), BF16)],
        compiler_params=_cparams(("arbitrary",)),
        name="norm_inproj",
    )(x2d, nw, w_in, log_lb, log_1m_lb)


def _level_ref(bcum, half):
    if half >= 8:
        b3 = bcum.reshape(CHUNK // (2 * half), 2 * half, HGRN_D)
        ref = jnp.broadcast_to(b3[:, half - 1:half, :], b3.shape)
        return ref.reshape(CHUNK, HGRN_D)
    b3 = bcum.reshape(CHUNK // 8, 8, HGRN_D)
    if half == 4:
        ref = jnp.broadcast_to(b3[:, 3:4, :], b3.shape)
    else:
        sub = lax.broadcasted_iota(jnp.int32, b3.shape, 1)
        ref = jnp.where(sub < 4,
                        jnp.broadcast_to(b3[:, 1:2, :], b3.shape),
                        jnp.broadcast_to(b3[:, 5:6, :], b3.shape))
    return ref.reshape(CHUNK, HGRN_D)


def _dot_nt(a, b):
    return lax.dot_general(a, b, (((1,), (1,)), ((), ())), preferred_element_type=F32)


def _dot_tn(a, b):
    return lax.dot_general(a, b, (((0,), (0,)), ((), ())), preferred_element_type=F32)


HGRN_LEVELS = (32, 16, 8, 4, 2)


def _hgrn_kernel(q_ref, lf_ref, bc_ref, v_ref, g_ref, gw_ref, o_ref, *st_refs):
    @pl.when(pl.program_id(1) == 0)
    def _():
        for st_ref in st_refs:
            st_ref[...] = jnp.zeros_like(st_ref)

    row = lax.broadcasted_iota(jnp.int32, (CHUNK, HGRN_D), 0)
    sign = {half: jnp.where((row & half) != 0, 1.0, -1.0) for half in HGRN_LEVELS}
    r64 = lax.broadcasted_iota(jnp.int32, (CHUNK, CHUNK), 0)
    c64 = lax.broadcasted_iota(jnp.int32, (CHUNK, CHUNK), 1)
    differ = jnp.where(r64 > c64, r64 ^ c64, 0)
    level = sum((differ >= half).astype(jnp.int32) for half in (1,) + HGRN_LEVELS)

    def chunk_body(c, carry):
        rows = pl.ds(pl.multiple_of(c * CHUNK, CHUNK), CHUNK)
        heads = range(HGRN_HEADS)
        cols = [slice(h * HGRN_D, (h + 1) * HGRN_D) for h in heads]
        q = [q_ref[rows, cs].astype(F32) for cs in cols]
        f = [jnp.exp2(lf_ref[rows, cs]) for cs in cols]
        k = [1.0 - x for x in f]
        bcum = [bc_ref[rows, cs] for cs in cols]

        scores = [_dot_nt((q[h] * f[h]).astype(BF16), k[h].astype(BF16)) for h in heads]
        scores = [jnp.where(level == 1, s, 0.0) for s in scores]
        for half in HGRN_LEVELS:
            lvl = int(math.log2(half)) + 1
            for h in heads:
                fac = jnp.exp2((bcum[h] - _level_ref(bcum[h], half)) * sign[half])
                s_l = _dot_nt((q[h] * fac).astype(BF16), (k[h] * fac).astype(BF16))
                scores[h] = jnp.where(level == lvl, s_l, scores[h])

        v = [v_ref[rows, cs] for cs in cols]
        st = [st_ref[...] for st_ref in st_refs]
        inter = [_dot_nt((q[h] * jnp.exp2(bcum[h])).astype(BF16), st[h].astype(BF16))
                 for h in heads]
        intra = [jnp.dot(scores[h].astype(BF16), v[h], preferred_element_type=F32)
                 for h in heads]
        for h in heads:
            b_last = bcum[h][CHUNK - 1:CHUNK, :]
            k_dec = (k[h] * jnp.exp2(b_last - bcum[h])).astype(BF16)
            st_refs[h][...] = st[h] * jnp.exp2(b_last) + _dot_tn(v[h], k_dec)
        for h in heads:
            diag = jnp.sum(q[h] * k[h], axis=-1, keepdims=True) * v[h].astype(F32)
            o = intra[h] + diag + inter[h]
            o = o * lax.rsqrt(jnp.mean(o * o, axis=-1, keepdims=True) + EPS)
            o = o * gw_ref[:, cols[h]] * g_ref[rows, cols[h]].astype(F32)
            o_ref[rows, cols[h]] = o.astype(BF16)
        return carry

    lax.fori_loop(0, HGRN_TB // CHUNK, chunk_body, 0, unroll=2)


def _hgrn(q, lf, bc, v, g, gnorm_w, batch, seq):
    blk = pl.BlockSpec((HGRN_TB, HGRN_WIDTH), lambda b, i: (b * (seq // HGRN_TB) + i, 0))
    return pl.pallas_call(
        _hgrn_kernel,
        grid=(batch, seq // HGRN_TB),
        in_specs=[blk] * 5 + [pl.BlockSpec((1, HGRN_WIDTH), lambda b, i: (0, 0))],
        out_specs=blk,
        out_shape=jax.ShapeDtypeStruct(q.shape, BF16),
        scratch_shapes=[pltpu.VMEM((HGRN_D, HGRN_D), F32)] * HGRN_HEADS,
        compiler_params=_cparams(("parallel", "arbitrary")),
        name="hgrn2",
    )(q, lf, bc, v, g, gnorm_w)


def _attn_kernel(bmax_ref, q_ref, k0_ref, k1_ref, k2_ref, v0_ref, v1_ref, v2_ref, bias_ref,
                 nw_ref, o_ref):
    i = pl.program_id(1)
    lane = lax.broadcasted_iota(jnp.int32, (1, LANES), 1)
    lo = lane < ATTN_HEAD_DIM

    def block(exact_max, pad_mask):
        den_min = None
        for pair in range(ATTN_HEADS // 2):
            cols = slice(pair * LANES, (pair + 1) * LANES)
            qp = q_ref[:, cols]
            kp = jnp.concatenate([k0_ref[:, cols], k1_ref[:, cols], k2_ref[:, cols]], axis=0)
            vp = jnp.concatenate([v0_ref[:, cols], v1_ref[:, cols], v2_ref[:, cols]], axis=0)
            if not exact_max:
                q_norm = jnp.sqrt(jnp.sum(jnp.square(qp.astype(F32)), axis=-1, keepdims=True))
                k_norm = jnp.sqrt(jnp.max(
                    jnp.sum(jnp.square(kp.astype(F32)), axis=-1, keepdims=True),
                    axis=0, keepdims=True))
                qk_bound = q_norm * k_norm
            outs = []
            for hh in range(2):
                head = 2 * pair + hh
                mine = lo if hh == 0 else ~lo
                qh = jnp.where(mine, qp, jnp.zeros_like(qp))
                s = _dot_nt(qh, kp) + bias_ref[head]
                if pad_mask is not None:
                    s = s + pad_mask
                if exact_max:
                    m = jnp.max(s, axis=-1, keepdims=True)
                else:
                    m = qk_bound + bmax_ref[head]
                e = jnp.exp2(s - m)
                pv = jnp.dot(e.astype(BF16), jnp.where(mine, vp, jnp.ones_like(vp)),
                             preferred_element_type=F32)
                outs.append(pv / pltpu.roll(pv, ATTN_HEAD_DIM, 1))
                if not exact_max:
                    den = jnp.where(mine, jnp.inf, pv)
                    den_min = den if den_min is None else jnp.minimum(den_min, den)
            o = jnp.where(lo, outs[0], outs[1])
            o2 = o * o
            ms0 = jnp.sum(jnp.where(lo, o2, 0.0), axis=-1, keepdims=True)
            ms1 = jnp.sum(jnp.where(lo, 0.0, o2), axis=-1, keepdims=True)
            ms = jnp.where(lo, ms0, ms1) * (1.0 / ATTN_HEAD_DIM)
            o_ref[:, cols] = (o * lax.rsqrt(ms + EPS) * nw_ref[:, cols]).astype(BF16)
        return den_min

    def run(pad_mask):
        den_min = block(False, pad_mask)
        safe = jnp.min(den_min) >= ATTN_MIN_DENOM

        @pl.when(jnp.logical_not(safe))
        def _():
            block(True, pad_mask)

    @pl.when(i >= 2)
    def _():
        run(None)

    @pl.when(i < 2)
    def _():
        col = lax.broadcasted_iota(jnp.int32, (1, ATTN_KB), 1)
        run(jnp.where(col >= (2 - i) * ATTN_QB, 0.0, NEG_BIG))


def _attention(aq, ak, av, bias_tab, bias_max, norm_w, batch, seq):
    nb = seq // ATTN_QB
    qblk = pl.BlockSpec((ATTN_QB, ATTN_WIDTH), lambda b, i: (b * nb + i, 0))

    def kblk(back):
        return pl.BlockSpec((ATTN_QB, ATTN_WIDTH),
                            lambda b, i: (b * nb + jnp.maximum(i - back, 0), 0))

    return pl.pallas_call(
        _attn_kernel,
        grid=(batch, nb),
        in_specs=[pl.BlockSpec(memory_space=pltpu.SMEM),
                  qblk, kblk(2), kblk(1), kblk(0), kblk(2), kblk(1), kblk(0),
                  pl.BlockSpec((ATTN_HEADS, ATTN_QB, ATTN_KB), lambda b, i: (0, 0, 0)),
                  pl.BlockSpec((1, ATTN_WIDTH), lambda b, i: (0, 0))],
        out_specs=qblk,
        out_shape=jax.ShapeDtypeStruct(aq.shape, BF16),
        compiler_params=_cparams(("parallel", "arbitrary")),
        name="band_attn",
    )(bias_max, aq, ak, ak, ak, av, av, av, bias_tab, norm_w)


BIAS_ROW = ATTN_QB + ATTN_KB


def _bias_kernel(base_ref, o_ref):
    row = lax.broadcasted_iota(jnp.int32, (ATTN_QB, BIAS_ROW), 0)
    x = jnp.broadcast_to(base_ref[0], (ATTN_QB, BIAS_ROW))
    x = pltpu.roll(x, BIAS_ROW - ATTN_QB, 1)
    for bit in range(int(math.log2(ATTN_QB))):
        x = jnp.where(((row >> bit) & 1) == 1, pltpu.roll(x, 1 << bit, 1), x)
    r = lax.broadcasted_iota(jnp.int32, (ATTN_QB, ATTN_KB), 0)
    c = lax.broadcasted_iota(jnp.int32, (ATTN_QB, ATTN_KB), 1)
    dchunk = (2 * ATTN_QB) // CHUNK + r // CHUNK - c // CHUNK
    valid = (dchunk >= 0) & (dchunk <= LEFT_CHUNKS)
    o_ref[0] = jnp.where(valid, x[:, :ATTN_KB] * LOG2_E, NEG_BIG)


def _bias_table(rel_bias):
    H = rel_bias.shape[0]
    n_hi = 2 * ATTN_QB - MAX_REL + ATTN_QB
    n_lo = BIAS_ROW - n_hi - (2 * MAX_REL + 1)
    rb = rel_bias.astype(F32)
    base = jnp.concatenate([jnp.broadcast_to(rb[:, -1:], (H, n_hi)), rb[:, ::-1],
                            jnp.broadcast_to(rb[:, :1], (H, n_lo))], axis=1)
    return pl.pallas_call(
        _bias_kernel,
        grid=(H,),
        in_specs=[pl.BlockSpec((1, 1, BIAS_ROW), lambda h: (h, 0, 0))],
        out_specs=pl.BlockSpec((1, ATTN_QB, ATTN_KB), lambda h: (h, 0, 0)),
        out_shape=jax.ShapeDtypeStruct((H, ATTN_QB, ATTN_KB), F32),
        compiler_params=_cparams(("parallel",)),
        name="rel_bias_table",
    )(base.reshape(H, 1, BIAS_ROW))


def _outproj_router_kernel(yr_ref, ya_ref, x_ref, wo_ref, nw_ref, wr_ref,
                           x1_ref, h2_ref, route_ref, cnt_ref, run_ref, wob_ref):
    @pl.when(pl.program_id(0) == 0)
    def _():
        run_ref[...] = jnp.zeros_like(run_ref)
        wob_ref[...] = wo_ref[...].astype(BF16)

    acc = jnp.dot(yr_ref[...], wob_ref[0:HGRN_WIDTH, :], preferred_element_type=F32)
    acc = acc + jnp.dot(ya_ref[...], wob_ref[HGRN_WIDTH:, :], preferred_element_type=F32)
    x1 = x_ref[...] + acc
    x1_ref[...] = x1
    h = x1 * lax.rsqrt(jnp.mean(x1 * x1, axis=-1, keepdims=True) + EPS) * nw_ref[...]
    h_hi = h.astype(BF16)
    h2_ref[:, :D_MODEL] = h_hi.astype(F32)

    h_lo = (h - h_hi.astype(F32)).astype(BF16)
    hi_prod = jnp.dot(h_hi, wr_ref[...], preferred_element_type=F32)
    lg = (hi_prod[:, :LANES] + hi_prod[:, LANES:]
          + jnp.dot(h_lo, wr_ref[:, :LANES], preferred_element_type=F32))
    lane = lax.broadcasted_iota(jnp.int32, lg.shape, 1)
    lanef = lane.astype(F32)
    is_g = (lane >= N_EXPERTS) & (lane < N_EXPERTS + N_GROUPS)
    g_max = jnp.max(jnp.where(is_g, lg, -jnp.inf), axis=-1, keepdims=True)
    g_sum = jnp.sum(jnp.where(is_g, jnp.exp(lg - g_max), 0.0), axis=-1, keepdims=True)
    g_gate = 1.0 / g_sum
    g_idx = jnp.min(jnp.where(is_g & (lg == g_max), lanef - N_EXPERTS, 1e9),
                    axis=-1, keepdims=True)
    in_grp = (lane < N_EXPERTS) & ((lane // EXPERTS_PER_GROUP).astype(F32) == g_idx)
    v1 = jnp.max(jnp.where(in_grp, lg, -jnp.inf), axis=-1, keepdims=True)
    i1 = jnp.min(jnp.where(in_grp & (lg == v1), lanef, 1e9), axis=-1, keepdims=True)
    rest = in_grp & (lanef != i1)
    v2 = jnp.max(jnp.where(rest, lg, -jnp.inf), axis=-1, keepdims=True)
    i2 = jnp.min(jnp.where(rest & (lg == v2), lanef, 1e9), axis=-1, keepdims=True)
    e2 = jnp.exp(v2 - v1)
    gate1 = g_gate / (1.0 + e2)
    gate2 = gate1 * e2

    hit = lanef == g_idx
    onehot = jnp.where(hit, 1.0, 0.0)
    n = lg.shape[0]
    rr = lax.broadcasted_iota(jnp.int32, (n, n), 0)
    cc = lax.broadcasted_iota(jnp.int32, (n, n), 1)
    earlier = jnp.where(cc < rr, 1.0, 0.0).astype(BF16)
    before = jnp.dot(earlier, onehot.astype(BF16), preferred_element_type=F32) + run_ref[...]
    rank = jnp.sum(jnp.where(hit, before, 0.0), axis=-1, keepdims=True)
    run_ref[...] += jnp.sum(onehot, axis=0, keepdims=True)
    cnt_ref[...] = run_ref[...]

    rec = jnp.zeros_like(lg)
    for slot, val in ((R_E1, i1), (R_E2, i2), (R_G1, gate1), (R_G2, gate2),
                      (R_GROUP, g_idx), (R_RANK, rank)):
        rec = jnp.where(lane == slot, val, rec)
    route_ref[...] = rec
    h2_ref[:, D_MODEL:] = rec


def _outproj_router(y_rec, y_att, x2d, w_out, layer, nw, w_router):
    T = x2d.shape[0]
    row = lambda i: (i, 0)
    const = lambda i: (0, 0)
    return pl.pallas_call(
        _outproj_router_kernel,
        grid=(T // TOK_TILE,),
        in_specs=[pl.BlockSpec((TOK_TILE, HGRN_WIDTH), row),
                  pl.BlockSpec((TOK_TILE, ATTN_WIDTH), row),
                  pl.BlockSpec((TOK_TILE, D_MODEL), row),
                  pl.BlockSpec((None, D_MODEL, D_MODEL), lambda i: (layer, 0, 0)),
                  pl.BlockSpec((1, D_MODEL), const),
                  pl.BlockSpec((D_MODEL, 2 * LANES), const)],
        out_specs=[pl.BlockSpec((TOK_TILE, D_MODEL), row),
                   pl.BlockSpec((TOK_TILE, ROW_W), row),
                   pl.BlockSpec((TOK_TILE, LANES), row),
                   pl.BlockSpec((1, LANES), const)],
        out_shape=[jax.ShapeDtypeStruct((T, D_MODEL), F32),
                   jax.ShapeDtypeStruct((T, ROW_W), F32),
                   jax.ShapeDtypeStruct((T, LANES), F32),
                   jax.ShapeDtypeStruct((1, LANES), F32)],
        scratch_shapes=[pltpu.VMEM((1, LANES), F32), pltpu.VMEM((D_MODEL, D_MODEL), BF16)],
        compiler_params=_cparams(("arbitrary",)),
        name="outproj_router",
    )(y_rec, y_att, x2d, w_out, nw, w_router)


def _row_copy(src_ref, src_row, dst_ref, dst_row, sem):
    return pltpu.make_async_copy(src_ref.at[pl.ds(src_row, 1), :],
                                 dst_ref.at[pl.ds(dst_row, 1), :], sem)


def _dispatch_kernel(pos_ref, h_ref, hs_init_ref, hs_ref, sem):
    del hs_init_ref

    def issue(t, carry):
        _row_copy(h_ref, t, hs_ref, pos_ref[t], sem).start()
        return carry

    lax.fori_loop(0, DISP_TILE, issue, 0, unroll=8)

    pltpu.make_async_copy(h_ref, hs_ref.at[pl.ds(0, DISP_TILE), :], sem).wait()


def _dispatch(pos, h2, n_rows):
    T = h2.shape[0]
    return pl.pallas_call(
        _dispatch_kernel,
        grid=(T // DISP_TILE,),
        in_specs=[pl.BlockSpec((DISP_TILE,), lambda i: (i,), memory_space=pltpu.SMEM),
                  pl.BlockSpec((DISP_TILE, ROW_W), lambda i: (i, 0)),
                  pl.BlockSpec(memory_space=pl.ANY)],
        out_specs=pl.BlockSpec(memory_space=pl.ANY),
        out_shape=jax.ShapeDtypeStruct((n_rows, ROW_W), F32),
        scratch_shapes=[pltpu.SemaphoreType.DMA(())],
        input_output_aliases={2: 0},
        compiler_params=_cparams(("arbitrary",)),
        name="moe_dispatch",
    )(pos, h2, jnp.zeros((n_rows, ROW_W), F32))


def _expert_kernel(tg_ref, tr_ref, nv_ref, hs_ref, w1_ref, w3_ref, w2_ref, ys_ref,
                   hb_ref, cw_ref):
    j = pl.program_id(0)
    e = pl.program_id(1)
    valid = j < nv_ref[0]

    @pl.when(valid & (e == 0))
    def _():
        hb_ref[...] = hs_ref[:, :D_MODEL].astype(BF16)
        rec = hs_ref[:, D_MODEL:]
        lane = lax.broadcasted_iota(jnp.int32, rec.shape, 1)

        def field(slot):
            return jnp.sum(jnp.where(lane == slot, rec, 0.0), axis=-1, keepdims=True)

        first = (tg_ref[j] * EXPERTS_PER_GROUP).astype(F32)
        lanef = lane.astype(F32)
        cw_ref[...] = (jnp.where(field(R_E1) - first == lanef, field(R_G1), 0.0)
                       + jnp.where(field(R_E2) - first == lanef, field(R_G2), 0.0))

    def experts_on(n_rows):
        h = hb_ref[:n_rows, :]
        cw = cw_ref[:n_rows, :]
        lane = lax.broadcasted_iota(jnp.int32, cw.shape, 1)
        hid = []
        for k in range(MOE_EPS):
            gate = jnp.sum(jnp.where(lane == e * MOE_EPS + k, cw, 0.0), axis=-1, keepdims=True)
            a = jnp.dot(h, w1_ref[k].astype(BF16), preferred_element_type=F32)
            b = jnp.dot(h, w3_ref[k].astype(BF16), preferred_element_type=F32)
            hid.append((_silu(a) * b * gate).astype(BF16))
        w2 = w2_ref[...].astype(BF16).reshape(MOE_EPS * EXPERT_FF, D_MODEL)
        y = jnp.dot(jnp.concatenate(hid, axis=-1), w2, preferred_element_type=F32)

        @pl.when(e == 0)
        def _():
            ys_ref[:n_rows, :] = y
            if n_rows < MOE_ROWS:
                ys_ref[n_rows:, :] = jnp.zeros((MOE_ROWS - n_rows, D_MODEL), F32)

        @pl.when(e > 0)
        def _():
            ys_ref[:n_rows, :] += y

    half_tile = MOE_ROWS // 2

    @pl.when(valid & (tr_ref[j] > half_tile))
    def _():
        experts_on(MOE_ROWS)

    @pl.when(valid & (tr_ref[j] <= half_tile))
    def _():
        experts_on(half_tile)

    @pl.when(jnp.logical_not(valid) & (e == 0))
    def _():
        ys_ref[...] = jnp.zeros_like(ys_ref)


def _experts(tile_group, tile_rows, n_valid, hs, w1, w3, w2, layer):
    n_rows = hs.shape[0]
    rows_in = lambda j, e, tg, tr, nv: (jnp.minimum(j, nv[0] - 1), 0)
    steps = EXPERTS_PER_GROUP // MOE_EPS

    def wsel(j, e, tg, tr, nv):
        last = nv[0] - 1
        blk = jnp.where(j <= last, tg[j] * steps + e, tg[last] * steps + steps - 1)
        return (layer, blk, 0, 0)

    return pl.pallas_call(
        _expert_kernel,
        grid_spec=pltpu.PrefetchScalarGridSpec(
            num_scalar_prefetch=3,
            grid=(n_rows // MOE_ROWS, steps),
            in_specs=[pl.BlockSpec((MOE_ROWS, ROW_W), rows_in),
                      pl.BlockSpec((None, MOE_EPS, D_MODEL, EXPERT_FF), wsel),
                      pl.BlockSpec((None, MOE_EPS, D_MODEL, EXPERT_FF), wsel),
                      pl.BlockSpec((None, MOE_EPS, EXPERT_FF, D_MODEL), wsel)],
            out_specs=pl.BlockSpec((MOE_ROWS, D_MODEL), lambda j, e, tg, tr, nv: (j, 0)),
            scratch_shapes=[pltpu.VMEM((MOE_ROWS, D_MODEL), BF16),
                            pltpu.VMEM((MOE_ROWS, LANES), F32)]),
        out_shape=jax.ShapeDtypeStruct((n_rows, D_MODEL), F32),
        compiler_params=_cparams(("arbitrary", "arbitrary")),
        name="moe_experts",
    )(tile_group, tile_rows, n_valid, hs, w1, w3, w2)


def _combine_kernel(pos_ref, x1_ref, fw_ref, ys_ref, o_ref, buf_ref, sem, *, final_norm):
    i = pl.program_id(0)
    n = pl.num_programs(0)

    def gather(tile, slot, start):
        if not start:
            pltpu.make_async_copy(ys_ref.at[pl.ds(0, COMB_TILE), :], buf_ref.at[slot],
                                  sem.at[slot]).wait()
            return
        base = tile * COMB_TILE

        def body(t, carry):
            _row_copy(ys_ref, pos_ref[base + t], buf_ref.at[slot], t, sem.at[slot]).start()
            return carry

        lax.fori_loop(0, COMB_TILE, body, 0, unroll=8)

    @pl.when(i == 0)
    def _():
        gather(0, 0, True)

    for slot in range(2):
        @pl.when((i + 1 < n) & ((i + 1) % 2 == slot))
        def _():
            gather(i + 1, slot, True)

    for slot in range(2):
        @pl.when(i % 2 == slot)
        def _():
            gather(i, slot, False)
            x2 = x1_ref[...] + buf_ref[slot]
            if final_norm:
                x2 = x2 * lax.rsqrt(jnp.mean(x2 * x2, axis=-1, keepdims=True) + EPS) * fw_ref[...]
            o_ref[...] = x2


def _combine(pos, x1, final_w, ys, final_norm):
    T = x1.shape[0]
    row = lambda i: (i, 0)
    return pl.pallas_call(
        functools.partial(_combine_kernel, final_norm=final_norm),
        grid=(T // COMB_TILE,),
        in_specs=[pl.BlockSpec(memory_space=pltpu.SMEM),
                  pl.BlockSpec((COMB_TILE, D_MODEL), row),
                  pl.BlockSpec((1, D_MODEL), lambda i: (0, 0)),
                  pl.BlockSpec(memory_space=pl.ANY)],
        out_specs=pl.BlockSpec((COMB_TILE, D_MODEL), row),
        out_shape=jax.ShapeDtypeStruct((T, D_MODEL), F32),
        scratch_shapes=[pltpu.VMEM((2, COMB_TILE, D_MODEL), F32),
                        pltpu.SemaphoreType.DMA((2,))],
        compiler_params=_cparams(("arbitrary",)),
        name="moe_combine",
    )(pos, x1, final_w, ys)


def _sorted_layout(route, counts, n_tiles):
    cnt = counts[0, :N_GROUPS].astype(jnp.int32)
    padded = (cnt + MOE_ROWS - 1) // MOE_ROWS * MOE_ROWS
    ends = jnp.cumsum(padded)
    offs = ends - padded
    tile_start = jnp.arange(n_tiles, dtype=jnp.int32) * MOE_ROWS
    tile_group = jnp.minimum(
        jnp.sum(ends[None, :] <= tile_start[:, None], axis=1), N_GROUPS - 1).astype(jnp.int32)
    tile_rows = jnp.clip(cnt[tile_group] - (tile_start - offs[tile_group]), 0, MOE_ROWS)
    n_valid = (ends[-1:] // MOE_ROWS).astype(jnp.int32)
    group = route[:, R_GROUP].astype(jnp.int32)
    ids = jnp.arange(N_GROUPS, dtype=jnp.int32)
    seg = jnp.sum(jnp.where(group[:, None] == ids[None, :], offs[None, :], 0), axis=1)
    return seg + route[:, R_RANK].astype(jnp.int32), tile_group, tile_rows, n_valid


def kernel(x, norm_mix_w, w_in, hgrn_lower_bounds, hgrn_gnorm_w, attn_rel_bias, attn_norm_w,
           w_out, norm_ffn_w, moe_w_group, moe_w_expert, moe_w1, moe_w3, moe_w2, final_norm_w):
    B, S, D = x.shape
    T = B * S
    lb_all = jnp.cumsum(jax.nn.softmax(hgrn_lower_bounds.astype(F32), axis=0), axis=0)
    lb_all = lb_all - lb_all[0]
    log_lb = jnp.log(lb_all)
    log_1m_lb = jnp.log1p(-lb_all)

    xt = x.reshape(T, D)
    for l in range(DEPTH):
        q, lf, bc, v, g, aq, ak, av = _norm_inproj(
            xt, norm_mix_w[l][None], w_in, l, log_lb[l][None], log_1m_lb[l][None])
        y_rec = _hgrn(q, lf, bc, v, g, hgrn_gnorm_w[l][None], B, S)
        bias_max = jnp.max(attn_rel_bias[l].astype(F32), axis=1) * LOG2_E
        y_att = _attention(aq, ak, av, _bias_table(attn_rel_bias[l]), bias_max,
                           attn_norm_w[l][None], B, S)
        w_r = jnp.concatenate(
            [moe_w_expert[l], moe_w_group[l],
             jnp.zeros((D, LANES - N_EXPERTS - N_GROUPS), F32)], axis=1)
        w_r_hi = w_r.astype(BF16)
        w_router = jnp.concatenate([w_r_hi, (w_r - w_r_hi.astype(F32)).astype(BF16)], axis=1)
        x1, h2, route, counts = _outproj_router(y_rec, y_att, xt, w_out, l,
                                                norm_ffn_w[l][None], w_router)
        n_tiles = T // MOE_ROWS + N_GROUPS
        pos, tile_group, tile_rows, n_valid = _sorted_layout(route, counts, n_tiles)
        hs = _dispatch(pos, h2, n_tiles * MOE_ROWS)
        ys = _experts(tile_group, tile_rows, n_valid, hs, moe_w1, moe_w3, moe_w2, l)
        xt = _combine(pos, x1, final_norm_w[None], ys, final_norm=(l == DEPTH - 1))
    return xt.reshape(B, S, D)
```

```python
import functools
import math

import jax
import jax.numpy as jnp
from jax import lax
from jax.experimental import pallas as pl
from jax.experimental.pallas import tpu as pltpu

F32 = jnp.float32
BF16 = jnp.bfloat16

D_MODEL = 1024
DEPTH = 2
CHUNK = 64
HGRN_WIDTH = 512
HGRN_HEADS = 4
HGRN_D = 128
ATTN_WIDTH = 512
ATTN_HEADS = 8
ATTN_HEAD_DIM = 64
LEFT_CHUNKS = 8
MAX_REL = 128
N_GROUPS = 4
EXPERTS_PER_GROUP = 8
N_EXPERTS = 32
EXPERT_FF = 256
IN_COLS = 4 * HGRN_WIDTH + 3 * ATTN_WIDTH
EPS = 1e-6

LANES = 128
VMEM_LIMIT = 56 * 1024 * 1024
NEG_BIG = -1e30
LOG2_E = 1.4426950408889634
ATTN_MIN_DENOM = 2.0 ** -90

TOK_TILE = 512
ATTN_QB = 256
ATTN_KB = 3 * ATTN_QB
HGRN_TB = 512
MOE_ROWS = 1024
MOE_EPS = 4
ROW_W = D_MODEL + LANES
DISP_TILE = 1024
COMB_TILE = 512
R_E1, R_E2, R_G1, R_G2, R_GROUP, R_RANK = range(6)


def _cparams(sem):
    return pltpu.CompilerParams(dimension_semantics=sem, vmem_limit_bytes=VMEM_LIMIT)


def _silu(x):
    return x * jax.nn.sigmoid(x)


def _chunk_cumsum(x, row_in_chunk):
    acc = x
    step = 1
    while step < CHUNK:
        acc = acc + jnp.where(row_in_chunk >= step, pltpu.roll(acc, step, 0), 0.0)
        step *= 2
    return acc


def _norm_inproj_kernel(x_ref, nw_ref, w_ref, la_ref, lc_ref,
                        q_ref, lf_ref, bc_ref, v_ref, g_ref, aq_ref, ak_ref, av_ref, wbf_ref):
    @pl.when(pl.program_id(0) == 0)
    def _():
        wbf_ref[...] = w_ref[...].astype(BF16)

    x = x_ref[...]
    ms = jnp.mean(x * x, axis=-1, keepdims=True)
    h = (x * lax.rsqrt(ms + EPS) * nw_ref[...]).astype(BF16)

    def proj(j):
        return jnp.dot(h, wbf_ref[:, j * 512:(j + 1) * 512], preferred_element_type=F32)

    q_ref[...] = _silu(proj(0)).astype(BF16)
    z = proj(1)
    log_sig = jnp.minimum(z, 0.0) - jnp.log(1.0 + jnp.exp(-jnp.abs(z)))
    a = la_ref[...]
    b = lc_ref[...] + log_sig
    lf2 = (jnp.maximum(a, b) + jnp.log(1.0 + jnp.exp(-jnp.abs(a - b)))) * LOG2_E
    lf_ref[...] = lf2
    row_in_chunk = lax.broadcasted_iota(jnp.int32, lf2.shape, 0) & (CHUNK - 1)
    bc_ref[...] = _chunk_cumsum(lf2, row_in_chunk)
    v_ref[...] = proj(2).astype(BF16)
    g_ref[...] = _silu(proj(3)).astype(BF16)
    aq_ref[...] = (proj(4) * (LOG2_E / math.sqrt(ATTN_HEAD_DIM))).astype(BF16)
    ak_ref[...] = proj(5).astype(BF16)
    av_ref[...] = proj(6).astype(BF16)


def _norm_inproj(x2d, nw, w_in, layer, log_lb, log_1m_lb):
    T = x2d.shape[0]
    row = lambda i: (i, 0)
    const = lambda i: (0, 0)
    out_bf = jax.ShapeDtypeStruct((T, 512), BF16)
    out_f32 = jax.ShapeDtypeStruct((T, 512), F32)
    blk = pl.BlockSpec((TOK_TILE, 512), row)
    return pl.pallas_call(
        _norm_inproj_kernel,
        grid=(T // TOK_TILE,),
        in_specs=[pl.BlockSpec((TOK_TILE, D_MODEL), row),
                  pl.BlockSpec((1, D_MODEL), const),
                  pl.BlockSpec((None, D_MODEL, IN_COLS), lambda i: (layer, 0, 0),
                               pipeline_mode=pl.Buffered(1)),
                  pl.BlockSpec((1, 512), const),
                  pl.BlockSpec((1, 512), const)],
        out_specs=[blk] * 8,
        out_shape=[out_bf, out_f32, out_f32, out_bf, out_bf, out_bf, out_bf, out_bf],
        scratch_shapes=[pltpu.VMEM((D_MODEL, IN_COLS), BF16)],
        compiler_params=_cparams(("arbitrary",)),
        name="norm_inproj",
    )(x2d, nw, w_in, log_lb, log_1m_lb)


def _level_ref(bcum, half):
    if half >= 8:
        b3 = bcum.reshape(CHUNK // (2 * half), 2 * half, HGRN_D)
        ref = jnp.broadcast_to(b3[:, half - 1:half, :], b3.shape)
        return ref.reshape(CHUNK, HGRN_D)
    b3 = bcum.reshape(CHUNK // 8, 8, HGRN_D)
    if half == 4:
        ref = jnp.broadcast_to(b3[:, 3:4, :], b3.shape)
    else:
        sub = lax.broadcasted_iota(jnp.int32, b3.shape, 1)
        ref = jnp.where(sub < 4,
                        jnp.broadcast_to(b3[:, 1:2, :], b3.shape),
                        jnp.broadcast_to(b3[:, 5:6, :], b3.shape))
    return ref.reshape(CHUNK, HGRN_D)


def _dot_nt(a, b):
    return lax.dot_general(a, b, (((1,), (1,)), ((), ())), preferred_element_type=F32)


def _dot_tn(a, b):
    return lax.dot_general(a, b, (((0,), (0,)), ((), ())), preferred_element_type=F32)


HGRN_LEVELS = (32, 16, 8, 4, 2)


def _hgrn_kernel(q_ref, lf_ref, bc_ref, v_ref, g_ref, gw_ref, o_ref, *st_refs):
    @pl.when(pl.program_id(1) == 0)
    def _():
        for st_ref in st_refs:
            st_ref[...] = jnp.zeros_like(st_ref)

    row = lax.broadcasted_iota(jnp.int32, (CHUNK, HGRN_D), 0)
    sign = {half: jnp.where((row & half) != 0, 1.0, -1.0) for half in HGRN_LEVELS}
    r64 = lax.broadcasted_iota(jnp.int32, (CHUNK, CHUNK), 0)
    c64 = lax.broadcasted_iota(jnp.int32, (CHUNK, CHUNK), 1)
    differ = jnp.where(r64 > c64, r64 ^ c64, 0)
    level = sum((differ >= half).astype(jnp.int32) for half in (1,) + HGRN_LEVELS)

    def chunk_body(c, carry):
        rows = pl.ds(pl.multiple_of(c * CHUNK, CHUNK), CHUNK)
        heads = range(HGRN_HEADS)
        cols = [slice(h * HGRN_D, (h + 1) * HGRN_D) for h in heads]
        q = [q_ref[rows, cs].astype(F32) for cs in cols]
        f = [jnp.exp2(lf_ref[rows, cs]) for cs in cols]
        k = [1.0 - x for x in f]
        bcum = [bc_ref[rows, cs] for cs in cols]

        scores = [_dot_nt((q[h] * f[h]).astype(BF16), k[h].astype(BF16)) for h in heads]
        scores = [jnp.where(level == 1, s, 0.0) for s in scores]
        for half in HGRN_LEVELS:
            lvl = int(math.log2(half)) + 1
            for h in heads:
                fac = jnp.exp2((bcum[h] - _level_ref(bcum[h], half)) * sign[half])
                s_l = _dot_nt((q[h] * fac).astype(BF16), (k[h] * fac).astype(BF16))
                scores[h] = jnp.where(level == lvl, s_l, scores[h])

        v = [v_ref[rows, cs] for cs in cols]
        st = [st_ref[...] for st_ref in st_refs]
        inter = [_dot_nt((q[h] * jnp.exp2(bcum[h])).astype(BF16), st[h].astype(BF16))
                 for h in heads]
        intra = [jnp.dot(scores[h].astype(BF16), v[h], preferred_element_type=F32)
                 for h in heads]
        for h in heads:
            b_last = bcum[h][CHUNK - 1:CHUNK, :]
            k_dec = (k[h] * jnp.exp2(b_last - bcum[h])).astype(BF16)
            st_refs[h][...] = st[h] * jnp.exp2(b_last) + _dot_tn(v[h], k_dec)
        for h in heads:
            diag = jnp.sum(q[h] * k[h], axis=-1, keepdims=True) * v[h].astype(F32)
            o = intra[h] + diag + inter[h]
            o = o * lax.rsqrt(jnp.mean(o * o, axis=-1, keepdims=True) + EPS)
            o = o * gw_ref[:, cols[h]] * g_ref[rows, cols[h]].astype(F32)
            o_ref[rows, cols[h]] = o.astype(BF16)
        return carry

    lax.fori_loop(0, HGRN_TB // CHUNK, chunk_body, 0, unroll=2)


def _hgrn(q, lf, bc, v, g, gnorm_w, batch, seq):
    blk = pl.BlockSpec((HGRN_TB, HGRN_WIDTH), lambda b, i: (b * (seq // HGRN_TB) + i, 0))
    return pl.pallas_call(
        _hgrn_kernel,
        grid=(batch, seq // HGRN_TB),
        in_specs=[blk] * 5 + [pl.BlockSpec((1, HGRN_WIDTH), lambda b, i: (0, 0))],
        out_specs=blk,
        out_shape=jax.ShapeDtypeStruct(q.shape, BF16),
        scratch_shapes=[pltpu.VMEM((HGRN_D, HGRN_D), F32)] * HGRN_HEADS,
        compiler_params=_cparams(("parallel", "arbitrary")),
        name="hgrn2",
    )(q, lf, bc, v, g, gnorm_w)


def _attn_kernel(bmax_ref, q_ref, k0_ref, k1_ref, k2_ref, v0_ref, v1_ref, v2_ref, bias_ref,
                 nw_ref, o_ref):
    i = pl.program_id(1)
    lane = lax.broadcasted_iota(jnp.int32, (1, LANES), 1)
    lo = lane < ATTN_HEAD_DIM

    def block(exact_max, pad_mask):
        den_min = None
        for pair in range(ATTN_HEADS // 2):
            cols = slice(pair * LANES, (pair + 1) * LANES)
            qp = q_ref[:, cols]
            kp = jnp.concatenate([k0_ref[:, cols], k1_ref[:, cols], k2_ref[:, cols]], axis=0)
            vp = jnp.concatenate([v0_ref[:, cols], v1_ref[:, cols], v2_ref[:, cols]], axis=0)
            if not exact_max:
                q_norm = jnp.sqrt(jnp.sum(jnp.square(qp.astype(F32)), axis=-1, keepdims=True))
                k_norm = jnp.sqrt(jnp.max(
                    jnp.sum(jnp.square(kp.astype(F32)), axis=-1, keepdims=True),
                    axis=0, keepdims=True))
                qk_bound = q_norm * k_norm
            outs = []
            for hh in range(2):
                head = 2 * pair + hh
                mine = lo if hh == 0 else ~lo
                qh = jnp.where(mine, qp, jnp.zeros_like(qp))
                s = _dot_nt(qh, kp) + bias_ref[head]
                if pad_mask is not None:
                    s = s + pad_mask
                if exact_max:
                    m = jnp.max(s, axis=-1, keepdims=True)
                else:
                    m = qk_bound + bmax_ref[head]
                e = jnp.exp2(s - m)
                pv = jnp.dot(e.astype(BF16), jnp.where(mine, vp, jnp.ones_like(vp)),
                             preferred_element_type=F32)
                outs.append(pv / pltpu.roll(pv, ATTN_HEAD_DIM, 1))
                if not exact_max:
                    den = jnp.where(mine, jnp.inf, pv)
                    den_min = den if den_min is None else jnp.minimum(den_min, den)
            o = jnp.where(lo, outs[0], outs[1])
            o2 = o * o
            ms0 = jnp.sum(jnp.where(lo, o2, 0.0), axis=-1, keepdims=True)
            ms1 = jnp.sum(jnp.where(lo, 0.0, o2), axis=-1, keepdims=True)
            ms = jnp.where(lo, ms0, ms1) * (1.0 / ATTN_HEAD_DIM)
            o_ref[:, cols] = (o * lax.rsqrt(ms + EPS) * nw_ref[:, cols]).astype(BF16)
        return den_min

    def run(pad_mask):
        den_min = block(False, pad_mask)
        safe = jnp.min(den_min) >= ATTN_MIN_DENOM

        @pl.when(jnp.logical_not(safe))
        def _():
            block(True, pad_mask)

    @pl.when(i >= 2)
    def _():
        run(None)

    @pl.when(i < 2)
    def _():
        col = lax.broadcasted_iota(jnp.int32, (1, ATTN_KB), 1)
        run(jnp.where(col >= (2 - i) * ATTN_QB, 0.0, NEG_BIG))


def _attention(aq, ak, av, bias_tab, bias_max, norm_w, batch, seq):
    nb = seq // ATTN_QB
    qblk = pl.BlockSpec((ATTN_QB, ATTN_WIDTH), lambda b, i: (b * nb + i, 0))

    def kblk(back):
        return pl.BlockSpec((ATTN_QB, ATTN_WIDTH),
                            lambda b, i: (b * nb + jnp.maximum(i - back, 0), 0))

    return pl.pallas_call(
        _attn_kernel,
        grid=(batch, nb),
        in_specs=[pl.BlockSpec(memory_space=pltpu.SMEM),
                  qblk, kblk(2), kblk(1), kblk(0), kblk(2), kblk(1), kblk(0),
                  pl.BlockSpec((ATTN_HEADS, ATTN_QB, ATTN_KB), lambda b, i: (0, 0, 0)),
                  pl.BlockSpec((1, ATTN_WIDTH), lambda b, i: (0, 0))],
        out_specs=qblk,
        out_shape=jax.ShapeDtypeStruct(aq.shape, BF16),
        compiler_params=_cparams(("parallel", "arbitrary")),
        name="band_attn",
    )(bias_max, aq, ak, ak, ak, av, av, av, bias_tab, norm_w)


BIAS_ROW = ATTN_QB + ATTN_KB


def _bias_kernel(base_ref, o_ref):
    row = lax.broadcasted_iota(jnp.int32, (ATTN_QB, BIAS_ROW), 0)
    x = jnp.broadcast_to(base_ref[0], (ATTN_QB, BIAS_ROW))
    x = pltpu.roll(x, BIAS_ROW - ATTN_QB, 1)
    for bit in range(int(math.log2(ATTN_QB))):
        x = jnp.where(((row >> bit) & 1) == 1, pltpu.roll(x, 1 << bit, 1), x)
    r = lax.broadcasted_iota(jnp.int32, (ATTN_QB, ATTN_KB), 0)
    c = lax.broadcasted_iota(jnp.int32, (ATTN_QB, ATTN_KB), 1)
    dchunk = (2 * ATTN_QB) // CHUNK + r // CHUNK - c // CHUNK
    valid = (dchunk >= 0) & (dchunk <= LEFT_CHUNKS)
    o_ref[0] = jnp.where(valid, x[:, :ATTN_KB] * LOG2_E, NEG_BIG)


def _bias_table(rel_bias):
    H = rel_bias.shape[0]
    n_hi = 2 * ATTN_QB - MAX_REL + ATTN_QB
    n_lo = BIAS_ROW - n_hi - (2 * MAX_REL + 1)
    rb = rel_bias.astype(F32)
    base = jnp.concatenate([jnp.broadcast_to(rb[:, -1:], (H, n_hi)), rb[:, ::-1],
                            jnp.broadcast_to(rb[:, :1], (H, n_lo))], axis=1)
    return pl.pallas_call(
        _bias_kernel,
        grid=(H,),
        in_specs=[pl.BlockSpec((1, 1, BIAS_ROW), lambda h: (h, 0, 0))],
        out_specs=pl.BlockSpec((1, ATTN_QB, ATTN_KB), lambda h: (h, 0, 0)),
        out_shape=jax.ShapeDtypeStruct((H, ATTN_QB, ATTN_KB), F32),
        compiler_params=_cparams(("parallel",)),
        name="rel_bias_table",
    )(base.reshape(H, 1, BIAS_ROW))


def _outproj_router_kernel(yr_ref, ya_ref, x_ref, wo_ref, nw_ref, wr_ref,
                           x1_ref, h2_ref, route_ref, cnt_ref, run_ref, wob_ref):
    @pl.when(pl.program_id(0) == 0)
    def _():
        run_ref[...] = jnp.zeros_like(run_ref)
        wob_ref[...] = wo_ref[...].astype(BF16)

    acc = jnp.dot(yr_ref[...], wob_ref[0:HGRN_WIDTH, :], preferred_element_type=F32)
    acc = acc + jnp.dot(ya_ref[...], wob_ref[HGRN_WIDTH:, :], preferred_element_type=F32)
    x1 = x_ref[...] + acc
    x1_ref[...] = x1
    h = x1 * lax.rsqrt(jnp.mean(x1 * x1, axis=-1, keepdims=True) + EPS) * nw_ref[...]
    h_hi = h.astype(BF16)
    h2_ref[:, :D_MODEL] = h_hi.astype(F32)

    h_lo = (h - h_hi.astype(F32)).astype(BF16)
    hi_prod = jnp.dot(h_hi, wr_ref[...], preferred_element_type=F32)
    lg = (hi_prod[:, :LANES] + hi_prod[:, LANES:]
          + jnp.dot(h_lo, wr_ref[:, :LANES], preferred_element_type=F32))
    lane = lax.broadcasted_iota(jnp.int32, lg.shape, 1)
    lanef = lane.astype(F32)
    is_g = (lane >= N_EXPERTS) & (lane < N_EXPERTS + N_GROUPS)
    g_max = jnp.max(jnp.where(is_g, lg, -jnp.inf), axis=-1, keepdims=True)
    g_sum = jnp.sum(jnp.where(is_g, jnp.exp(lg - g_max), 0.0), axis=-1, keepdims=True)
    g_gate = 1.0 / g_sum
    g_idx = jnp.min(jnp.where(is_g & (lg == g_max), lanef - N_EXPERTS, 1e9),
                    axis=-1, keepdims=True)
    in_grp = (lane < N_EXPERTS) & ((lane // EXPERTS_PER_GROUP).astype(F32) == g_idx)
    v1 = jnp.max(jnp.where(in_grp, lg, -jnp.inf), axis=-1, keepdims=True)
    i1 = jnp.min(jnp.where(in_grp & (lg == v1), lanef, 1e9), axis=-1, keepdims=True)
    rest = in_grp & (lanef != i1)
    v2 = jnp.max(jnp.where(rest, lg, -jnp.inf), axis=-1, keepdims=True)
    i2 = jnp.min(jnp.where(rest & (lg == v2), lanef, 1e9), axis=-1, keepdims=True)
    e2 = jnp.exp(v2 - v1)
    gate1 = g_gate / (1.0 + e2)
    gate2 = gate1 * e2

    hit = lanef == g_idx
    onehot = jnp.where(hit, 1.0, 0.0)
    n = lg.shape[0]
    rr = lax.broadcasted_iota(jnp.int32, (n, n), 0)
    cc = lax.broadcasted_iota(jnp.int32, (n, n), 1)
    earlier = jnp.where(cc < rr, 1.0, 0.0).astype(BF16)
    before = jnp.dot(earlier, onehot.astype(BF16), preferred_element_type=F32) + run_ref[...]
    rank = jnp.sum(jnp.where(hit, before, 0.0), axis=-1, keepdims=True)
    run_ref[...] += jnp.sum(onehot, axis=0, keepdims=True)
    cnt_ref[...] = run_ref[...]

    rec = jnp.zeros_like(lg)
    for slot, val in ((R_E1, i1), (R_E2, i2), (R_G1, gate1), (R_G2, gate2),
                      (R_GROUP, g_idx), (R_RANK, rank)):
        rec = jnp.where(lane == slot, val, rec)
    route_ref[...] = rec
    h2_ref[:, D_MODEL:] = rec


def _outproj_router(y_rec, y_att, x2d, w_out, layer, nw, w_router):
    T = x2d.shape[0]
    row = lambda i: (i, 0)
    const = lambda i: (0, 0)
    return pl.pallas_call(
        _outproj_router_kernel,
        grid=(T // TOK_TILE,),
        in_specs=[pl.BlockSpec((TOK_TILE, HGRN_WIDTH), row),
                  pl.BlockSpec((TOK_TILE, ATTN_WIDTH), row),
                  pl.BlockSpec((TOK_TILE, D_MODEL), row),
                  pl.BlockSpec((None, D_MODEL, D_MODEL), lambda i: (layer, 0, 0)),
                  pl.BlockSpec((1, D_MODEL), const),
                  pl.BlockSpec((D_MODEL, 2 * LANES), const)],
        out_specs=[pl.BlockSpec((TOK_TILE, D_MODEL), row),
                   pl.BlockSpec((TOK_TILE, ROW_W), row),
                   pl.BlockSpec((TOK_TILE, LANES), row),
                   pl.BlockSpec((1, LANES), const)],
        out_shape=[jax.ShapeDtypeStruct((T, D_MODEL), F32),
                   jax.ShapeDtypeStruct((T, ROW_W), F32),
                   jax.ShapeDtypeStruct((T, LANES), F32),
                   jax.ShapeDtypeStruct((1, LANES), F32)],
        scratch_shapes=[pltpu.VMEM((1, LANES), F32), pltpu.VMEM((D_MODEL, D_MODEL), BF16)],
        compiler_params=_cparams(("arbitrary",)),
        name="outproj_router",
    )(y_rec, y_att, x2d, w_out, nw, w_router)


def _row_copy(src_ref, src_row, dst_ref, dst_row, sem):
    return pltpu.make_async_copy(src_ref.at[pl.ds(src_row, 1), :],
                                 dst_ref.at[pl.ds(dst_row, 1), :], sem)


def _dispatch_kernel(pos_ref, h_ref, hs_init_ref, hs_ref, sem):
    del hs_init_ref

    def issue(t, carry):
        _row_copy(h_ref, t, hs_ref, pos_ref[t], sem).start()
        return carry

    lax.fori_loop(0, DISP_TILE, issue, 0, unroll=8)

    pltpu.make_async_copy(h_ref, hs_ref.at[pl.ds(0, DISP_TILE), :], sem).wait()


def _dispatch(pos, h2, n_rows):
    T = h2.shape[0]
    return pl.pallas_call(
        _dispatch_kernel,
        grid=(T // DISP_TILE,),
        in_specs=[pl.BlockSpec((DISP_TILE,), lambda i: (i,), memory_space=pltpu.SMEM),
                  pl.BlockSpec((DISP_TILE, ROW_W), lambda i: (i, 0)),
                  pl.BlockSpec(memory_space=pl.ANY)],
        out_specs=pl.BlockSpec(memory_space=pl.ANY),
        out_shape=jax.ShapeDtypeStruct((n_rows, ROW_W), F32),
        scratch_shapes=[pltpu.SemaphoreType.DMA(())],
        input_output_aliases={2: 0},
        compiler_params=_cparams(("arbitrary",)),
        name="moe_dispatch",
    )(pos, h2, jnp.zeros((n_rows, ROW_W), F32))


def _expert_kernel(tg_ref, tr_ref, nv_ref, hs_ref, w1_ref, w3_ref, w2_ref, ys_ref,
                   hb_ref, cw_ref):
    j = pl.program_id(0)
    e = pl.program_id(1)
    valid = j < nv_ref[0]

    def experts_on(n_rows, first_step):
        lane = lax.broadcasted_iota(jnp.int32, (n_rows, LANES), 1)
        if first_step:
            h = hs_ref[:n_rows, :D_MODEL].astype(BF16)
            hb_ref[:n_rows, :] = h
            rec = hs_ref[:n_rows, D_MODEL:]

            def field(slot):
                return jnp.sum(jnp.where(lane == slot, rec, 0.0), axis=-1, keepdims=True)

            first = (tg_ref[j] * EXPERTS_PER_GROUP).astype(F32)
            lanef = lane.astype(F32)
            cw = (jnp.where(field(R_E1) - first == lanef, field(R_G1), 0.0)
                  + jnp.where(field(R_E2) - first == lanef, field(R_G2), 0.0))
            cw_ref[:n_rows, :] = cw
        else:
            h = hb_ref[:n_rows, :]
            cw = cw_ref[:n_rows, :]
        hid = []
        for k in range(MOE_EPS):
            gate = jnp.sum(jnp.where(lane == e * MOE_EPS + k, cw, 0.0), axis=-1, keepdims=True)
            a = jnp.dot(h, w1_ref[k].astype(BF16), preferred_element_type=F32)
            b = jnp.dot(h, w3_ref[k].astype(BF16), preferred_element_type=F32)
            hid.append((_silu(a) * b * gate).astype(BF16))
        w2 = w2_ref[...].astype(BF16).reshape(MOE_EPS * EXPERT_FF, D_MODEL)
        y = jnp.dot(jnp.concatenate(hid, axis=-1), w2, preferred_element_type=F32)

        if first_step:
            ys_ref[:n_rows, :] = y
            if n_rows < MOE_ROWS:
                ys_ref[n_rows:, :] = jnp.zeros((MOE_ROWS - n_rows, D_MODEL), F32)
        else:
            ys_ref[:n_rows, :] += y

    half_tile = MOE_ROWS // 2
    for first_step in (True, False):
        at_step = (e == 0) if first_step else (e > 0)
        pl.when(valid & at_step & (tr_ref[j] > half_tile))(
            functools.partial(experts_on, MOE_ROWS, first_step))
        pl.when(valid & at_step & (tr_ref[j] <= half_tile))(
            functools.partial(experts_on, half_tile, first_step))

    @pl.when(jnp.logical_not(valid) & (e == 0))
    def _():
        ys_ref[...] = jnp.zeros_like(ys_ref)


def _experts(tile_group, tile_rows, n_valid, hs, w1, w3, w2, layer):
    n_rows = hs.shape[0]
    rows_in = lambda j, e, tg, tr, nv: (jnp.minimum(j, nv[0] - 1), 0)
    steps = EXPERTS_PER_GROUP // MOE_EPS

    def wsel(j, e, tg, tr, nv):
        last = nv[0] - 1
        blk = jnp.where(j <= last, tg[j] * steps + e, tg[last] * steps + steps - 1)
        return (layer, blk, 0, 0)

    return pl.pallas_call(
        _expert_kernel,
        grid_spec=pltpu.PrefetchScalarGridSpec(
            num_scalar_prefetch=3,
            grid=(n_rows // MOE_ROWS, steps),
            in_specs=[pl.BlockSpec((MOE_ROWS, ROW_W), rows_in),
                      pl.BlockSpec((None, MOE_EPS, D_MODEL, EXPERT_FF), wsel),
                      pl.BlockSpec((None, MOE_EPS, D_MODEL, EXPERT_FF), wsel),
                      pl.BlockSpec((None, MOE_EPS, EXPERT_FF, D_MODEL), wsel)],
            out_specs=pl.BlockSpec((MOE_ROWS, D_MODEL), lambda j, e, tg, tr, nv: (j, 0)),
            scratch_shapes=[pltpu.VMEM((MOE_ROWS, D_MODEL), BF16),
                            pltpu.VMEM((MOE_ROWS, LANES), F32)]),
        out_shape=jax.ShapeDtypeStruct((n_rows, D_MODEL), F32),
        compiler_params=_cparams(("arbitrary", "arbitrary")),
        name="moe_experts",
    )(tile_group, tile_rows, n_valid, hs, w1, w3, w2)


def _combine_kernel(pos_ref, x1_ref, fw_ref, ys_ref, o_ref, buf_ref, sem, *, final_norm):
    i = pl.program_id(0)
    n = pl.num_programs(0)

    def gather(tile, slot, start):
        if not start:
            pltpu.make_async_copy(ys_ref.at[pl.ds(0, COMB_TILE), :], buf_ref.at[slot],
                                  sem.at[slot]).wait()
            return
        base = tile * COMB_TILE

        def body(t, carry):
            _row_copy(ys_ref, pos_ref[base + t], buf_ref.at[slot], t, sem.at[slot]).start()
            return carry

        lax.fori_loop(0, COMB_TILE, body, 0, unroll=8)

    @pl.when(i == 0)
    def _():
        gather(0, 0, True)

    for slot in range(2):
        @pl.when((i + 1 < n) & ((i + 1) % 2 == slot))
        def _():
            gather(i + 1, slot, True)

    for slot in range(2):
        @pl.when(i % 2 == slot)
        def _():
            gather(i, slot, False)
            x2 = x1_ref[...] + buf_ref[slot]
            if final_norm:
                x2 = x2 * lax.rsqrt(jnp.mean(x2 * x2, axis=-1, keepdims=True) + EPS) * fw_ref[...]
            o_ref[...] = x2


def _combine(pos, x1, final_w, ys, final_norm):
    T = x1.shape[0]
    row = lambda i: (i, 0)
    return pl.pallas_call(
        functools.partial(_combine_kernel, final_norm=final_norm),
        grid=(T // COMB_TILE,),
        in_specs=[pl.BlockSpec(memory_space=pltpu.SMEM),
                  pl.BlockSpec((COMB_TILE, D_MODEL), row),
                  pl.BlockSpec((1, D_MODEL), lambda i: (0, 0)),
                  pl.BlockSpec(memory_space=pl.ANY)],
        out_specs=pl.BlockSpec((COMB_TILE, D_MODEL), row),
        out_shape=jax.ShapeDtypeStruct((T, D_MODEL), F32),
        scratch_shapes=[pltpu.VMEM((2, COMB_TILE, D_MODEL), F32),
                        pltpu.SemaphoreType.DMA((2,))],
        compiler_params=_cparams(("arbitrary",)),
        name="moe_combine",
    )(pos, x1, final_w, ys)


def _sorted_layout(route, counts, n_tiles):
    cnt = counts[0, :N_GROUPS].astype(jnp.int32)
    padded = (cnt + MOE_ROWS - 1) // MOE_ROWS * MOE_ROWS
    ends = jnp.cumsum(padded)
    offs = ends - padded
    tile_start = jnp.arange(n_tiles, dtype=jnp.int32) * MOE_ROWS
    tile_group = jnp.minimum(
        jnp.sum(ends[None, :] <= tile_start[:, None], axis=1), N_GROUPS - 1).astype(jnp.int32)
    tile_rows = jnp.clip(cnt[tile_group] - (tile_start - offs[tile_group]), 0, MOE_ROWS)
    n_valid = (ends[-1:] // MOE_ROWS).astype(jnp.int32)
    group = route[:, R_GROUP].astype(jnp.int32)
    ids = jnp.arange(N_GROUPS, dtype=jnp.int32)
    seg = jnp.sum(jnp.where(group[:, None] == ids[None, :], offs[None, :], 0), axis=1)
    return seg + route[:, R_RANK].astype(jnp.int32), tile_group, tile_rows, n_valid


def kernel(x, norm_mix_w, w_in, hgrn_lower_bounds, hgrn_gnorm_w, attn_rel_bias, attn_norm_w,
           w_out, norm_ffn_w, moe_w_group, moe_w_expert, moe_w1, moe_w3, moe_w2, final_norm_w):
    B, S, D = x.shape
    T = B * S
    lb_all = jnp.cumsum(jax.nn.softmax(hgrn_lower_bounds.astype(F32), axis=0), axis=0)
    lb_all = lb_all - lb_all[0]
    log_lb = jnp.log(lb_all)
    log_1m_lb = jnp.log1p(-lb_all)

    xt = x.reshape(T, D)
    for l in range(DEPTH):
        q, lf, bc, v, g, aq, ak, av = _norm_inproj(
            xt, norm_mix_w[l][None], w_in, l, log_lb[l][None], log_1m_lb[l][None])
        y_rec = _hgrn(q, lf, bc, v, g, hgrn_gnorm_w[l][None], B, S)
        bias_max = jnp.max(attn_rel_bias[l].astype(F32), axis=1) * LOG2_E
        y_att = _attention(aq, ak, av, _bias_table(attn_rel_bias[l]), bias_max,
                           attn_norm_w[l][None], B, S)
        w_r = jnp.concatenate(
            [moe_w_expert[l], moe_w_group[l],
             jnp.zeros((D, LANES - N_EXPERTS - N_GROUPS), F32)], axis=1)
        w_r_hi = w_r.astype(BF16)
        w_router = jnp.concatenate([w_r_hi, (w_r - w_r_hi.astype(F32)).astype(BF16)], axis=1)
        x1, h2, route, counts = _outproj_router(y_rec, y_att, xt, w_out, l,
                                                norm_ffn_w[l][None], w_router)
        n_tiles = T // MOE_ROWS + N_GROUPS
        pos, tile_group, tile_rows, n_valid = _sorted_layout(route, counts, n_tiles)
        hs = _dispatch(pos, h2, n_tiles * MOE_ROWS)
        ys = _experts(tile_group, tile_rows, n_valid, hs, moe_w1, moe_w3, moe_w2, l)
        xt = _combine(pos, x1, final_norm_w[None], ys, final_norm=(l == DEPTH - 1))
    return xt.reshape(B, S, D)
```

```python
import functools
import math

import jax
import jax.numpy as jnp
from jax import lax
from jax.experimental import pallas as pl
from jax.experimental.pallas import tpu as pltpu

F32 = jnp.float32
BF16 = jnp.bfloat16

D_MODEL = 1024
DEPTH = 2
CHUNK = 64
HGRN_WIDTH = 512
HGRN_HEADS = 4
HGRN_D = 128
ATTN_WIDTH = 512
ATTN_HEADS = 8
ATTN_HEAD_DIM = 64
LEFT_CHUNKS = 8
MAX_REL = 128
N_GROUPS = 4
EXPERTS_PER_GROUP = 8
N_EXPERTS = 32
EXPERT_FF = 256
IN_COLS = 4 * HGRN_WIDTH + 3 * ATTN_WIDTH
EPS = 1e-6

LANES = 128
SUBLANES = 8
VMEM_LIMIT = 56 * 1024 * 1024
NEG_BIG = -1e30
LOG2_E = 1.4426950408889634
ATTN_MIN_DENOM = 2.0 ** -90

TOK_TILE = 512
ATTN_QB = 256
ATTN_KB = 3 * ATTN_QB
HGRN_TB = 512
MOE_ROWS = 1024
MOE_EPS = 4
ROW_W = D_MODEL + LANES
DISP_TILE = 1024
COMB_TILE = 512
R_E1, R_E2, R_G1, R_G2, R_GROUP, R_RANK = range(6)


def _cparams(sem):
    return pltpu.CompilerParams(dimension_semantics=sem, vmem_limit_bytes=VMEM_LIMIT)


def _silu(x):
    return x * jax.nn.sigmoid(x)


def _chunk_cumsum(x):
    n, w = x.shape
    groups = CHUNK // SUBLANES
    x4 = x.reshape(n // CHUNK, groups, SUBLANES, w)
    sub = lax.broadcasted_iota(jnp.int32, x4.shape, 2)
    step = 1
    while step < SUBLANES:
        x4 = x4 + jnp.where(sub >= step, pltpu.roll(x4, step, 2), 0.0)
        step *= 2
    parts = [x4[:, 0]]
    for g in range(1, groups):
        parts.append(x4[:, g] + parts[-1][:, SUBLANES - 1:SUBLANES, :])
    return jnp.stack(parts, axis=1).reshape(n, w)


def _norm_inproj_kernel(x_ref, nw_ref, w_ref, la_ref, lc_ref,
                        q_ref, lf_ref, bc_ref, v_ref, g_ref, aq_ref, ak_ref, av_ref, wbf_ref):
    @pl.when(pl.program_id(0) == 0)
    def _():
        wbf_ref[...] = w_ref[...].astype(BF16)

    x = x_ref[...]
    ms = jnp.mean(x * x, axis=-1, keepdims=True)
    h = (x * lax.rsqrt(ms + EPS) * nw_ref[...]).astype(BF16)

    def proj(j):
        return jnp.dot(h, wbf_ref[:, j * 512:(j + 1) * 512], preferred_element_type=F32)

    q_ref[...] = _silu(proj(0)).astype(BF16)
    z = proj(1)
    log_sig = jnp.minimum(z, 0.0) - jnp.log(1.0 + jnp.exp(-jnp.abs(z)))
    a = la_ref[...]
    b = lc_ref[...] + log_sig
    lf2 = (jnp.maximum(a, b) + jnp.log(1.0 + jnp.exp(-jnp.abs(a - b)))) * LOG2_E
    lf_ref[...] = lf2
    bc_ref[...] = _chunk_cumsum(lf2)
    v_ref[...] = proj(2).astype(BF16)
    g_ref[...] = _silu(proj(3)).astype(BF16)
    aq_ref[...] = (proj(4) * (LOG2_E / math.sqrt(ATTN_HEAD_DIM))).astype(BF16)
    ak_ref[...] = proj(5).astype(BF16)
    av_ref[...] = proj(6).astype(BF16)


def _norm_inproj(x2d, nw, w_in, layer, log_lb, log_1m_lb):
    T = x2d.shape[0]
    row = lambda i: (i, 0)
    const = lambda i: (0, 0)
    out_bf = jax.ShapeDtypeStruct((T, 512), BF16)
    out_f32 = jax.ShapeDtypeStruct((T, 512), F32)
    blk = pl.BlockSpec((TOK_TILE, 512), row)
    return pl.pallas_call(
        _norm_inproj_kernel,
        grid=(T // TOK_TILE,),
        in_specs=[pl.BlockSpec((TOK_TILE, D_MODEL), row),
                  pl.BlockSpec((1, D_MODEL), const),
                  pl.BlockSpec((None, D_MODEL, IN_COLS), lambda i: (layer, 0, 0),
                               pipeline_mode=pl.Buffered(1)),
                  pl.BlockSpec((1, 512), const),
                  pl.BlockSpec((1, 512), const)],
        out_specs=[blk] * 8,
        out_shape=[out_bf, out_f32, out_f32, out_bf, out_bf, out_bf, out_bf, out_bf],
        scratch_shapes=[pltpu.VMEM((D_MODEL, IN_COLS), BF16)],
        compiler_params=_cparams(("arbitrary",)),
        name="norm_inproj",
    )(x2d, nw, w_in, log_lb, log_1m_lb)


def _level_ref(bcum, half):
    if half >= 8:
        b3 = bcum.reshape(CHUNK // (2 * half), 2 * half, HGRN_D)
        ref = jnp.broadcast_to(b3[:, half - 1:half, :], b3.shape)
        return ref.reshape(CHUNK, HGRN_D)
    b3 = bcum.reshape(CHUNK // 8, 8, HGRN_D)
    if half == 4:
        ref = jnp.broadcast_to(b3[:, 3:4, :], b3.shape)
    else:
        sub = lax.broadcasted_iota(jnp.int32, b3.shape, 1)
        ref = jnp.where(sub < 4,
                        jnp.broadcast_to(b3[:, 1:2, :], b3.shape),
                        jnp.broadcast_to(b3[:, 5:6, :], b3.shape))
    return ref.reshape(CHUNK, HGRN_D)


def _dot_nt(a, b):
    return lax.dot_general(a, b, (((1,), (1,)), ((), ())), preferred_element_type=F32)


def _dot_tn(a, b):
    return lax.dot_general(a, b, (((0,), (0,)), ((), ())), preferred_element_type=F32)


HGRN_LEVELS = (32, 16, 8, 4, 2)


def _hgrn_kernel(q_ref, lf_ref, bc_ref, v_ref, g_ref, gw_ref, o_ref, *st_refs):
    @pl.when(pl.program_id(1) == 0)
    def _():
        for st_ref in st_refs:
            st_ref[...] = jnp.zeros_like(st_ref)

    row = lax.broadcasted_iota(jnp.int32, (CHUNK, HGRN_D), 0)
    sign = {half: jnp.where((row & half) != 0, 1.0, -1.0) for half in HGRN_LEVELS}
    r64 = lax.broadcasted_iota(jnp.int32, (CHUNK, CHUNK), 0)
    c64 = lax.broadcasted_iota(jnp.int32, (CHUNK, CHUNK), 1)
    differ = jnp.where(r64 > c64, r64 ^ c64, 0)
    level = sum((differ >= half).astype(jnp.int32) for half in (1,) + HGRN_LEVELS)

    def chunk_body(c, carry):
        rows = pl.ds(pl.multiple_of(c * CHUNK, CHUNK), CHUNK)
        heads = range(HGRN_HEADS)
        cols = [slice(h * HGRN_D, (h + 1) * HGRN_D) for h in heads]
        q = [q_ref[rows, cs].astype(F32) for cs in cols]
        f = [jnp.exp2(lf_ref[rows, cs]) for cs in cols]
        k = [1.0 - x for x in f]
        bcum = [bc_ref[rows, cs] for cs in cols]

        scores = [_dot_nt((q[h] * f[h]).astype(BF16), k[h].astype(BF16)) for h in heads]
        scores = [jnp.where(level == 1, s, 0.0) for s in scores]
        for half in HGRN_LEVELS:
            lvl = int(math.log2(half)) + 1
            for h in heads:
                fac = jnp.exp2((bcum[h] - _level_ref(bcum[h], half)) * sign[half])
                s_l = _dot_nt((q[h] * fac).astype(BF16), (k[h] * fac).astype(BF16))
                scores[h] = jnp.where(level == lvl, s_l, scores[h])

        v = [v_ref[rows, cs] for cs in cols]
        st = [st_ref[...] for st_ref in st_refs]
        inter = [_dot_nt((q[h] * jnp.exp2(bcum[h])).astype(BF16), st[h].astype(BF16))
                 for h in heads]
        intra = [jnp.dot(scores[h].astype(BF16), v[h], preferred_element_type=F32)
                 for h in heads]
        for h in heads:
            b_last = bcum[h][CHUNK - 1:CHUNK, :]
            k_dec = (k[h] * jnp.exp2(b_last - bcum[h])).astype(BF16)
            st_refs[h][...] = st[h] * jnp.exp2(b_last) + _dot_tn(v[h], k_dec)
        for h in heads:
            diag = jnp.sum(q[h] * k[h], axis=-1, keepdims=True) * v[h].astype(F32)
            o = intra[h] + diag + inter[h]
            o = o * lax.rsqrt(jnp.mean(o * o, axis=-1, keepdims=True) + EPS)
            o = o * gw_ref[:, cols[h]] * g_ref[rows, cols[h]].astype(F32)
            o_ref[rows, cols[h]] = o.astype(BF16)
        return carry

    lax.fori_loop(0, HGRN_TB // CHUNK, chunk_body, 0, unroll=2)


def _hgrn(q, lf, bc, v, g, gnorm_w, batch, seq):
    blk = pl.BlockSpec((HGRN_TB, HGRN_WIDTH), lambda b, i: (b * (seq // HGRN_TB) + i, 0))
    return pl.pallas_call(
        _hgrn_kernel,
        grid=(batch, seq // HGRN_TB),
        in_specs=[blk] * 5 + [pl.BlockSpec((1, HGRN_WIDTH), lambda b, i: (0, 0))],
        out_specs=blk,
        out_shape=jax.ShapeDtypeStruct(q.shape, BF16),
        scratch_shapes=[pltpu.VMEM((HGRN_D, HGRN_D), F32)] * HGRN_HEADS,
        compiler_params=_cparams(("parallel", "arbitrary")),
        name="hgrn2",
    )(q, lf, bc, v, g, gnorm_w)


def _attn_kernel(bmax_ref, q_ref, k0_ref, k1_ref, k2_ref, v0_ref, v1_ref, v2_ref, bias_ref,
                 nw_ref, o_ref):
    i = pl.program_id(1)
    lane = lax.broadcasted_iota(jnp.int32, (1, LANES), 1)
    lo = lane < ATTN_HEAD_DIM

    def block(exact_max, pad_mask):
        den_min = None
        for pair in range(ATTN_HEADS // 2):
            cols = slice(pair * LANES, (pair + 1) * LANES)
            qp = q_ref[:, cols]
            kp = jnp.concatenate([k0_ref[:, cols], k1_ref[:, cols], k2_ref[:, cols]], axis=0)
            vp = jnp.concatenate([v0_ref[:, cols], v1_ref[:, cols], v2_ref[:, cols]], axis=0)
            if not exact_max:
                q_norm = jnp.sqrt(jnp.sum(jnp.square(qp.astype(F32)), axis=-1, keepdims=True))
                k_norm = jnp.sqrt(jnp.max(
                    jnp.sum(jnp.square(kp.astype(F32)), axis=-1, keepdims=True),
                    axis=0, keepdims=True))
                qk_bound = q_norm * k_norm
            outs = []
            for hh in range(2):
                head = 2 * pair + hh
                mine = lo if hh == 0 else ~lo
                qh = jnp.where(mine, qp, jnp.zeros_like(qp))
                s = _dot_nt(qh, kp) + bias_ref[head]
                if pad_mask is not None:
                    s = s + pad_mask
                if exact_max:
                    m = jnp.max(s, axis=-1, keepdims=True)
                else:
                    m = qk_bound + bmax_ref[head]
                e = jnp.exp2(s - m)
                pv = jnp.dot(e.astype(BF16), jnp.where(mine, vp, jnp.ones_like(vp)),
                             preferred_element_type=F32)
                outs.append(pv / pltpu.roll(pv, ATTN_HEAD_DIM, 1))
                if not exact_max:
                    den = jnp.where(mine, jnp.inf, pv)
                    den_min = den if den_min is None else jnp.minimum(den_min, den)
            o = jnp.where(lo, outs[0], outs[1])
            o2 = o * o
            ms0 = jnp.sum(jnp.where(lo, o2, 0.0), axis=-1, keepdims=True)
            ms1 = jnp.sum(jnp.where(lo, 0.0, o2), axis=-1, keepdims=True)
            ms = jnp.where(lo, ms0, ms1) * (1.0 / ATTN_HEAD_DIM)
            o_ref[:, cols] = (o * lax.rsqrt(ms + EPS) * nw_ref[:, cols]).astype(BF16)
        return den_min

    def run(pad_mask):
        den_min = block(False, pad_mask)
        safe = jnp.min(den_min) >= ATTN_MIN_DENOM

        @pl.when(jnp.logical_not(safe))
        def _():
            block(True, pad_mask)

    @pl.when(i >= 2)
    def _():
        run(None)

    @pl.when(i < 2)
    def _():
        col = lax.broadcasted_iota(jnp.int32, (1, ATTN_KB), 1)
        run(jnp.where(col >= (2 - i) * ATTN_QB, 0.0, NEG_BIG))


def _attention(aq, ak, av, bias_tab, bias_max, norm_w, batch, seq):
    nb = seq // ATTN_QB
    qblk = pl.BlockSpec((ATTN_QB, ATTN_WIDTH), lambda b, i: (b * nb + i, 0))

    def kblk(back):
        return pl.BlockSpec((ATTN_QB, ATTN_WIDTH),
                            lambda b, i: (b * nb + jnp.maximum(i - back, 0), 0))

    return pl.pallas_call(
        _attn_kernel,
        grid=(batch, nb),
        in_specs=[pl.BlockSpec(memory_space=pltpu.SMEM),
                  qblk, kblk(2), kblk(1), kblk(0), kblk(2), kblk(1), kblk(0),
                  pl.BlockSpec((ATTN_HEADS, ATTN_QB, ATTN_KB), lambda b, i: (0, 0, 0)),
                  pl.BlockSpec((1, ATTN_WIDTH), lambda b, i: (0, 0))],
        out_specs=qblk,
        out_shape=jax.ShapeDtypeStruct(aq.shape, BF16),
        compiler_params=_cparams(("parallel", "arbitrary")),
        name="band_attn",
    )(bias_max, aq, ak, ak, ak, av, av, av, bias_tab, norm_w)


BIAS_ROW = ATTN_QB + ATTN_KB


def _bias_kernel(base_ref, o_ref):
    row = lax.broadcasted_iota(jnp.int32, (ATTN_QB, BIAS_ROW), 0)
    x = jnp.broadcast_to(base_ref[0], (ATTN_QB, BIAS_ROW))
    x = pltpu.roll(x, BIAS_ROW - ATTN_QB, 1)
    for bit in range(int(math.log2(ATTN_QB))):
        x = jnp.where(((row >> bit) & 1) == 1, pltpu.roll(x, 1 << bit, 1), x)
    r = lax.broadcasted_iota(jnp.int32, (ATTN_QB, ATTN_KB), 0)
    c = lax.broadcasted_iota(jnp.int32, (ATTN_QB, ATTN_KB), 1)
    dchunk = (2 * ATTN_QB) // CHUNK + r // CHUNK - c // CHUNK
    valid = (dchunk >= 0) & (dchunk <= LEFT_CHUNKS)
    o_ref[0] = jnp.where(valid, x[:, :ATTN_KB] * LOG2_E, NEG_BIG)


def _bias_table(rel_bias):
    H = rel_bias.shape[0]
    n_hi = 2 * ATTN_QB - MAX_REL + ATTN_QB
    n_lo = BIAS_ROW - n_hi - (2 * MAX_REL + 1)
    rb = rel_bias.astype(F32)
    base = jnp.concatenate([jnp.broadcast_to(rb[:, -1:], (H, n_hi)), rb[:, ::-1],
                            jnp.broadcast_to(rb[:, :1], (H, n_lo))], axis=1)
    return pl.pallas_call(
        _bias_kernel,
        grid=(H,),
        in_specs=[pl.BlockSpec((1, 1, BIAS_ROW), lambda h: (h, 0, 0))],
        out_specs=pl.BlockSpec((1, ATTN_QB, ATTN_KB), lambda h: (h, 0, 0)),
        out_shape=jax.ShapeDtypeStruct((H, ATTN_QB, ATTN_KB), F32),
        compiler_params=_cparams(("parallel",)),
        name="rel_bias_table",
    )(base.reshape(H, 1, BIAS_ROW))


def _outproj_router_kernel(yr_ref, ya_ref, x_ref, wo_ref, nw_ref, wr_ref,
                           x1_ref, h2_ref, route_ref, cnt_ref, run_ref, wob_ref):
    @pl.when(pl.program_id(0) == 0)
    def _():
        run_ref[...] = jnp.zeros_like(run_ref)
        wob_ref[...] = wo_ref[...].astype(BF16)

    acc = jnp.dot(yr_ref[...], wob_ref[0:HGRN_WIDTH, :], preferred_element_type=F32)
    acc = acc + jnp.dot(ya_ref[...], wob_ref[HGRN_WIDTH:, :], preferred_element_type=F32)
    x1 = x_ref[...] + acc
    x1_ref[...] = x1
    h = x1 * lax.rsqrt(jnp.mean(x1 * x1, axis=-1, keepdims=True) + EPS) * nw_ref[...]
    h_hi = h.astype(BF16)
    h2_ref[:, :D_MODEL] = h_hi.astype(F32)

    h_lo = (h - h_hi.astype(F32)).astype(BF16)
    hi_prod = jnp.dot(h_hi, wr_ref[...], preferred_element_type=F32)
    lg = (hi_prod[:, :LANES] + hi_prod[:, LANES:]
          + jnp.dot(h_lo, wr_ref[:, :LANES], preferred_element_type=F32))
    lane = lax.broadcasted_iota(jnp.int32, lg.shape, 1)
    lanef = lane.astype(F32)
    is_g = (lane >= N_EXPERTS) & (lane < N_EXPERTS + N_GROUPS)
    g_max = jnp.max(jnp.where(is_g, lg, -jnp.inf), axis=-1, keepdims=True)
    g_sum = jnp.sum(jnp.where(is_g, jnp.exp(lg - g_max), 0.0), axis=-1, keepdims=True)
    g_gate = 1.0 / g_sum
    g_idx = jnp.min(jnp.where(is_g & (lg == g_max), lanef - N_EXPERTS, 1e9),
                    axis=-1, keepdims=True)
    in_grp = (lane < N_EXPERTS) & ((lane // EXPERTS_PER_GROUP).astype(F32) == g_idx)
    v1 = jnp.max(jnp.where(in_grp, lg, -jnp.inf), axis=-1, keepdims=True)
    i1 = jnp.min(jnp.where(in_grp & (lg == v1), lanef, 1e9), axis=-1, keepdims=True)
    rest = in_grp & (lanef != i1)
    v2 = jnp.max(jnp.where(rest, lg, -jnp.inf), axis=-1, keepdims=True)
    i2 = jnp.min(jnp.where(rest & (lg == v2), lanef, 1e9), axis=-1, keepdims=True)
    e2 = jnp.exp(v2 - v1)
    gate1 = g_gate / (1.0 + e2)
    gate2 = gate1 * e2

    hit = lanef == g_idx
    onehot = jnp.where(hit, 1.0, 0.0)
    n = lg.shape[0]
    rr = lax.broadcasted_iota(jnp.int32, (n, n), 0)
    cc = lax.broadcasted_iota(jnp.int32, (n, n), 1)
    earlier = jnp.where(cc < rr, 1.0, 0.0).astype(BF16)
    before = jnp.dot(earlier, onehot.astype(BF16), preferred_element_type=F32) + run_ref[...]
    rank = jnp.sum(jnp.where(hit, before, 0.0), axis=-1, keepdims=True)
    run_ref[...] += jnp.sum(onehot, axis=0, keepdims=True)
    cnt_ref[...] = run_ref[...]

    rec = jnp.zeros_like(lg)
    for slot, val in ((R_E1, i1), (R_E2, i2), (R_G1, gate1), (R_G2, gate2),
                      (R_GROUP, g_idx), (R_RANK, rank)):
        rec = jnp.where(lane == slot, val, rec)
    route_ref[...] = rec
    h2_ref[:, D_MODEL:] = rec


def _outproj_router(y_rec, y_att, x2d, w_out, layer, nw, w_router):
    T = x2d.shape[0]
    row = lambda i: (i, 0)
    const = lambda i: (0, 0)
    return pl.pallas_call(
        _outproj_router_kernel,
        grid=(T // TOK_TILE,),
        in_specs=[pl.BlockSpec((TOK_TILE, HGRN_WIDTH), row),
                  pl.BlockSpec((TOK_TILE, ATTN_WIDTH), row),
                  pl.BlockSpec((TOK_TILE, D_MODEL), row),
                  pl.BlockSpec((None, D_MODEL, D_MODEL), lambda i: (layer, 0, 0)),
                  pl.BlockSpec((1, D_MODEL), const),
                  pl.BlockSpec((D_MODEL, 2 * LANES), const)],
        out_specs=[pl.BlockSpec((TOK_TILE, D_MODEL), row),
                   pl.BlockSpec((TOK_TILE, ROW_W), row),
                   pl.BlockSpec((TOK_TILE, LANES), row),
                   pl.BlockSpec((1, LANES), const)],
        out_shape=[jax.ShapeDtypeStruct((T, D_MODEL), F32),
                   jax.ShapeDtypeStruct((T, ROW_W), F32),
                   jax.ShapeDtypeStruct((T, LANES), F32),
                   jax.ShapeDtypeStruct((1, LANES), F32)],
        scratch_shapes=[pltpu.VMEM((1, LANES), F32), pltpu.VMEM((D_MODEL, D_MODEL), BF16)],
        compiler_params=_cparams(("arbitrary",)),
        name="outproj_router",
    )(y_rec, y_att, x2d, w_out, nw, w_router)


def _row_copy(src_ref, src_row, dst_ref, dst_row, sem):
    return pltpu.make_async_copy(src_ref.at[pl.ds(src_row, 1), :],
                                 dst_ref.at[pl.ds(dst_row, 1), :], sem)


def _dispatch_kernel(pad_ref, pos_ref, h_ref, hs_ref, zeros_ref, sem, zero_sem):
    @pl.when(pl.program_id(0) == 0)
    def _():
        zeros_ref[...] = jnp.zeros_like(zeros_ref)

        def fill(j):
            return pltpu.make_async_copy(
                zeros_ref, hs_ref.at[pl.ds(j * MOE_ROWS, MOE_ROWS), :], zero_sem)

        n_tiles = hs_ref.shape[0] // MOE_ROWS
        for j in range(n_tiles):
            pl.when(pad_ref[j] != 0)(fill(j).start)
        for j in range(n_tiles):
            pl.when(pad_ref[j] != 0)(fill(j).wait)

    def issue(t, carry):
        _row_copy(h_ref, t, hs_ref, pos_ref[t], sem).start()
        return carry

    lax.fori_loop(0, DISP_TILE, issue, 0, unroll=8)

    pltpu.make_async_copy(h_ref, hs_ref.at[pl.ds(0, DISP_TILE), :], sem).wait()


def _dispatch(pos, tile_has_pad, h2, n_rows):
    T = h2.shape[0]
    return pl.pallas_call(
        _dispatch_kernel,
        grid=(T // DISP_TILE,),
        in_specs=[pl.BlockSpec(memory_space=pltpu.SMEM),
                  pl.BlockSpec((DISP_TILE,), lambda i: (i,), memory_space=pltpu.SMEM),
                  pl.BlockSpec((DISP_TILE, ROW_W), lambda i: (i, 0))],
        out_specs=pl.BlockSpec(memory_space=pl.ANY),
        out_shape=jax.ShapeDtypeStruct((n_rows, ROW_W), F32),
        scratch_shapes=[pltpu.VMEM((MOE_ROWS, ROW_W), F32),
                        pltpu.SemaphoreType.DMA(()), pltpu.SemaphoreType.DMA(())],
        compiler_params=_cparams(("arbitrary",)),
        name="moe_dispatch",
    )(tile_has_pad, pos, h2)


def _expert_kernel(tg_ref, tr_ref, nv_ref, hs_ref, w1_ref, w3_ref, w2_ref, ys_ref,
                   hb_ref, cw_ref):
    j = pl.program_id(0)
    e = pl.program_id(1)
    valid = j < nv_ref[0]

    def experts_on(n_rows, first_step):
        lane = lax.broadcasted_iota(jnp.int32, (n_rows, LANES), 1)
        if first_step:
            h = hs_ref[:n_rows, :D_MODEL].astype(BF16)
            hb_ref[:n_rows, :] = h
            rec = hs_ref[:n_rows, D_MODEL:]

            def field(slot):
                return jnp.sum(jnp.where(lane == slot, rec, 0.0), axis=-1, keepdims=True)

            first = (tg_ref[j] * EXPERTS_PER_GROUP).astype(F32)
            lanef = lane.astype(F32)
            cw = (jnp.where(field(R_E1) - first == lanef, field(R_G1), 0.0)
                  + jnp.where(field(R_E2) - first == lanef, field(R_G2), 0.0))
            cw_ref[:n_rows, :] = cw
        else:
            h = hb_ref[:n_rows, :]
            cw = cw_ref[:n_rows, :]
        hid = []
        for k in range(MOE_EPS):
            gate = jnp.sum(jnp.where(lane == e * MOE_EPS + k, cw, 0.0), axis=-1, keepdims=True)
            a = jnp.dot(h, w1_ref[k].astype(BF16), preferred_element_type=F32)
            b = jnp.dot(h, w3_ref[k].astype(BF16), preferred_element_type=F32)
            hid.append((_silu(a) * b * gate).astype(BF16))
        w2 = w2_ref[...].astype(BF16).reshape(MOE_EPS * EXPERT_FF, D_MODEL)
        y = jnp.dot(jnp.concatenate(hid, axis=-1), w2, preferred_element_type=F32)

        if first_step:
            ys_ref[:n_rows, :] = y
            if n_rows < MOE_ROWS:
                ys_ref[n_rows:, :] = jnp.zeros((MOE_ROWS - n_rows, D_MODEL), F32)
        else:
            ys_ref[:n_rows, :] += y

    half_tile = MOE_ROWS // 2
    for first_step in (True, False):
        at_step = (e == 0) if first_step else (e > 0)
        pl.when(valid & at_step & (tr_ref[j] > half_tile))(
            functools.partial(experts_on, MOE_ROWS, first_step))
        pl.when(valid & at_step & (tr_ref[j] <= half_tile))(
            functools.partial(experts_on, half_tile, first_step))

    @pl.when(jnp.logical_not(valid) & (e == 0))
    def _():
        ys_ref[...] = jnp.zeros_like(ys_ref)


def _experts(tile_group, tile_rows, n_valid, hs, w1, w3, w2, layer):
    n_rows = hs.shape[0]
    rows_in = lambda j, e, tg, tr, nv: (jnp.minimum(j, nv[0] - 1), 0)
    steps = EXPERTS_PER_GROUP // MOE_EPS

    def wsel(j, e, tg, tr, nv):
        last = nv[0] - 1
        blk = jnp.where(j <= last, tg[j] * steps + e, tg[last] * steps + steps - 1)
        return (layer, blk, 0, 0)

    return pl.pallas_call(
        _expert_kernel,
        grid_spec=pltpu.PrefetchScalarGridSpec(
            num_scalar_prefetch=3,
            grid=(n_rows // MOE_ROWS, steps),
            in_specs=[pl.BlockSpec((MOE_ROWS, ROW_W), rows_in),
                      pl.BlockSpec((None, MOE_EPS, D_MODEL, EXPERT_FF), wsel),
                      pl.BlockSpec((None, MOE_EPS, D_MODEL, EXPERT_FF), wsel),
                      pl.BlockSpec((None, MOE_EPS, EXPERT_FF, D_MODEL), wsel)],
            out_specs=pl.BlockSpec((MOE_ROWS, D_MODEL), lambda j, e, tg, tr, nv: (j, 0)),
            scratch_shapes=[pltpu.VMEM((MOE_ROWS, D_MODEL), BF16),
                            pltpu.VMEM((MOE_ROWS, LANES), F32)]),
        out_shape=jax.ShapeDtypeStruct((n_rows, D_MODEL), F32),
        compiler_params=_cparams(("arbitrary", "arbitrary")),
        name="moe_experts",
    )(tile_group, tile_rows, n_valid, hs, w1, w3, w2)


def _combine_kernel(pos_ref, x1_ref, fw_ref, ys_ref, o_ref, buf_ref, sem, *, final_norm):
    i = pl.program_id(0)
    n = pl.num_programs(0)

    def gather(tile, slot, start):
        if not start:
            pltpu.make_async_copy(ys_ref.at[pl.ds(0, COMB_TILE), :], buf_ref.at[slot],
                                  sem.at[slot]).wait()
            return
        base = tile * COMB_TILE

        def body(t, carry):
            _row_copy(ys_ref, pos_ref[base + t], buf_ref.at[slot], t, sem.at[slot]).start()
            return carry

        lax.fori_loop(0, COMB_TILE, body, 0, unroll=8)

    @pl.when(i == 0)
    def _():
        gather(0, 0, True)

    for slot in range(2):
        @pl.when((i + 1 < n) & ((i + 1) % 2 == slot))
        def _():
            gather(i + 1, slot, True)

    for slot in range(2):
        @pl.when(i % 2 == slot)
        def _():
            gather(i, slot, False)
            x2 = x1_ref[...] + buf_ref[slot]
            if final_norm:
                x2 = x2 * lax.rsqrt(jnp.mean(x2 * x2, axis=-1, keepdims=True) + EPS) * fw_ref[...]
            o_ref[...] = x2


def _combine(pos, x1, final_w, ys, final_norm):
    T = x1.shape[0]
    row = lambda i: (i, 0)
    return pl.pallas_call(
        functools.partial(_combine_kernel, final_norm=final_norm),
        grid=(T // COMB_TILE,),
        in_specs=[pl.BlockSpec(memory_space=pltpu.SMEM),
                  pl.BlockSpec((COMB_TILE, D_MODEL), row),
                  pl.BlockSpec((1, D_MODEL), lambda i: (0, 0)),
                  pl.BlockSpec(memory_space=pl.ANY)],
        out_specs=pl.BlockSpec((COMB_TILE, D_MODEL), row),
        out_shape=jax.ShapeDtypeStruct((T, D_MODEL), F32),
        scratch_shapes=[pltpu.VMEM((2, COMB_TILE, D_MODEL), F32),
                        pltpu.SemaphoreType.DMA((2,))],
        compiler_params=_cparams(("arbitrary",)),
        name="moe_combine",
    )(pos, x1, final_w, ys)


def _sorted_layout(route, counts, n_tiles):
    cnt = counts[0, :N_GROUPS].astype(jnp.int32)
    padded = (cnt + MOE_ROWS - 1) // MOE_ROWS * MOE_ROWS
    ends = jnp.cumsum(padded)
    offs = ends - padded
    tile_start = jnp.arange(n_tiles, dtype=jnp.int32) * MOE_ROWS
    tile_group = jnp.minimum(
        jnp.sum(ends[None, :] <= tile_start[:, None], axis=1), N_GROUPS - 1).astype(jnp.int32)
    tile_rows = jnp.clip(cnt[tile_group] - (tile_start - offs[tile_group]), 0, MOE_ROWS)
    n_valid = (ends[-1:] // MOE_ROWS).astype(jnp.int32)
    group = route[:, R_GROUP].astype(jnp.int32)
    ids = jnp.arange(N_GROUPS, dtype=jnp.int32)
    seg = jnp.sum(jnp.where(group[:, None] == ids[None, :], offs[None, :], 0), axis=1)
    return seg + route[:, R_RANK].astype(jnp.int32), tile_group, tile_rows, n_valid


def kernel(x, norm_mix_w, w_in, hgrn_lower_bounds, hgrn_gnorm_w, attn_rel_bias, attn_norm_w,
           w_out, norm_ffn_w, moe_w_group, moe_w_expert, moe_w1, moe_w3, moe_w2, final_norm_w):
    B, S, D = x.shape
    T = B * S
    lb_all = jnp.cumsum(jax.nn.softmax(hgrn_lower_bounds.astype(F32), axis=0), axis=0)
    lb_all = lb_all - lb_all[0]
    log_lb = jnp.log(lb_all)
    log_1m_lb = jnp.log1p(-lb_all)

    xt = x.reshape(T, D)
    for l in range(DEPTH):
        q, lf, bc, v, g, aq, ak, av = _norm_inproj(
            xt, norm_mix_w[l][None], w_in, l, log_lb[l][None], log_1m_lb[l][None])
        y_rec = _hgrn(q, lf, bc, v, g, hgrn_gnorm_w[l][None], B, S)
        bias_max = jnp.max(attn_rel_bias[l].astype(F32), axis=1) * LOG2_E
        y_att = _attention(aq, ak, av, _bias_table(attn_rel_bias[l]), bias_max,
                           attn_norm_w[l][None], B, S)
        w_r = jnp.concatenate(
            [moe_w_expert[l], moe_w_group[l],
             jnp.zeros((D, LANES - N_EXPERTS - N_GROUPS), F32)], axis=1)
        w_r_hi = w_r.astype(BF16)
        w_router = jnp.concatenate([w_r_hi, (w_r - w_r_hi.astype(F32)).astype(BF16)], axis=1)
        x1, h2, route, counts = _outproj_router(y_rec, y_att, xt, w_out, l,
                                                norm_ffn_w[l][None], w_router)
        n_tiles = T // MOE_ROWS + N_GROUPS
        pos, tile_group, tile_rows, n_valid = _sorted_layout(route, counts, n_tiles)
        tile_has_pad = (tile_rows < MOE_ROWS).astype(jnp.int32)
        hs = _dispatch(pos, tile_has_pad, h2, n_tiles * MOE_ROWS)
        ys = _experts(tile_group, tile_rows, n_valid, hs, moe_w1, moe_w3, moe_w2, l)
        xt = _combine(pos, x1, final_norm_w[None], ys, final_norm=(l == DEPTH - 1))
    return xt.reshape(B, S, D)
```
